```python
import math
import jax, jax.numpy as jnp
from jax import lax
import numpy as np

D_MODEL = 2048
BATCH = 4
SEQ = 2048
DEPTH = 2
DEC_BATCH = 128
DEC_SEQ = 4
PAST_LEN = 16384
PAGE_SIZE = 128

D_PLE = 256
D_A = D_MODEL // 2
CONV_A = 3
H_B = 8
DK = 128
DV = 128
D_B = H_B * DV
CONV_B = 4
DELTA_CHUNK = 64
D_C = D_MODEL // 2
CHUNK_C = 128
G_C = 8
C_GROUP_DIM = D_C // G_C
D_FF = ((8 * D_MODEL // 3 + 127) // 128) * 128
CONV_F = 3
N_BRANCH = 3
DEEPNORM_ALPHA = (2 * DEPTH) ** 0.25
DEEPNORM_BETA = (8 * DEPTH) ** -0.25
LN_EPS = 1e-5
RMS_EPS = 1e-6
SPLIT_SIZES = (D_A, D_A, D_A, 3 * D_B, D_B, H_B, H_B, D_C, D_C, N_BRANCH * D_MODEL)
N_IN = sum(SPLIT_SIZES)

kernel_name = 'hybrid_conv_delta_gmlp_deepnorm_step'


def layer_norm(x, g, b):
    xf = x.astype(jnp.float32)
    mu = jnp.mean(xf, -1, keepdims=True)
    xc = xf - mu
    var = jnp.mean(xc * xc, -1, keepdims=True)
    y = xc * lax.rsqrt(var + LN_EPS) * g.astype(jnp.float32) + b.astype(jnp.float32)
    return y.astype(x.dtype)


def rms_norm_f32(x, g):
    xf = x.astype(jnp.float32)
    return xf * lax.rsqrt(jnp.mean(xf * xf, -1, keepdims=True) + RMS_EPS) * g.astype(jnp.float32)


def l2_normalize(t):
    return t * lax.rsqrt(jnp.sum(t * t, -1, keepdims=True) + RMS_EPS)


def split_cols(t, sizes):
    offs = np.cumsum(np.array(sizes))[:-1].tolist()
    return jnp.split(t, offs, axis=-1)


def causal_dwconv(x, hist, w):
    width = w.shape[0]
    L = x.shape[1]
    xf = jnp.concatenate([hist, x], axis=1)
    out = xf[:, 0:L] * w[0]
    for j in range(1, width):
        out = out + xf[:, j:j + L] * w[j]
    return out, xf[:, xf.shape[1] - (width - 1):]


def gated_delta_rule(q, k, v, g, beta, s0):
    Bn, L, H, _ = q.shape
    C = min(DELTA_CHUNK, L)
    pad = (-L) % C
    n_chunks = (L + pad) // C

    def prep(t):
        t = jnp.pad(t, [(0, 0), (0, pad)] + [(0, 0)] * (t.ndim - 2))
        t = t.reshape((Bn, n_chunks, C) + t.shape[2:])
        return jnp.transpose(t, (1, 0, 3, 2) + tuple(range(4, t.ndim)))

    qc, kc, vc, gc, bc = prep(q), prep(k), prep(v), prep(g), prep(beta)
    gcum = jnp.cumsum(gc, axis=-1)
    idx = jnp.arange(C)
    causal = idx[:, None] >= idx[None, :]
    strict = idx[:, None] > idx[None, :]
    decay = jnp.exp(jnp.where(causal, gcum[..., :, None] - gcum[..., None, :], -jnp.inf))
    kb = kc * bc[..., None]
    vb = vc * bc[..., None]
    lmat = jnp.where(strict, jnp.einsum('nbhcd,nbhed->nbhce', kb, kc) * decay, 0.0)
    eye = jnp.eye(C, dtype=jnp.float32)
    tmat = lax.linalg.triangular_solve(eye + lmat, jnp.broadcast_to(eye, lmat.shape),
                                       left_side=True, lower=True, unit_diagonal=True)
    u = jnp.einsum('nbhce,nbhef->nbhcf', tmat, vb)
    w = jnp.einsum('nbhce,nbhed->nbhcd', tmat, kb * jnp.exp(gcum)[..., None])
    qk = jnp.einsum('nbhcd,nbhed->nbhce', qc, kc) * decay
    q_dec = qc * jnp.exp(gcum)[..., None]
    k_dec = kc * jnp.exp(gcum[..., -1:] - gcum)[..., None]
    g_last = jnp.exp(gcum[..., -1])

    def step(s, xs):
        u_n, w_n, qk_n, qd_n, kd_n, gl_n = xs
        v_new = u_n - jnp.einsum('bhcd,bhde->bhce', w_n, s)
        o_n = jnp.einsum('bhcd,bhde->bhce', qd_n, s) + jnp.einsum('bhce,bhef->bhcf', qk_n, v_new)
        s = s * gl_n[..., None, None] + jnp.einsum('bhcd,bhce->bhde', kd_n, v_new)
        return s, o_n

    s_fin, o = lax.scan(step, s0, (u, w, qk, q_dec, k_dec, g_last))
    o = jnp.transpose(o, (1, 0, 3, 2, 4)).reshape(Bn, n_chunks * C, H, DV)[:, :L]
    return o, s_fin


def chunk_spatial_mix(v, w_s, b_s):
    Bn, L, _ = v.shape
    pad = (-L) % CHUNK_C
    vp = jnp.pad(v, ((0, 0), (0, pad), (0, 0))).reshape(Bn, -1, CHUNK_C, G_C, C_GROUP_DIM)
    tri = jnp.tril(jnp.ones((CHUNK_C, CHUNK_C), dtype=bool))
    wm = jnp.where(tri, w_s, 0.0)
    mixed = jnp.einsum('gts,bnsgc->bntgc', wm, vp) + jnp.transpose(b_s)[None, None, :, :, None]
    return mixed.reshape(Bn, -1, D_C)[:, :L]


def layer_step(x, p, hist_a, hist_qkv, s_delta, hist_f,
               w_in, conv_a_w, w_a_out, conv_b_w, a_log, dt_bias, norm_b_g, w_b_out,
               ln_c_g, ln_c_b, w_s, b_s, w_c_out, w_o, ln1_g, ln1_b,
               w_up, conv_f_w, w_down, w_pe, w_pg, ln2_g, ln2_b):
    Bn, L, _ = x.shape
    proj = x @ w_in
    a_h, a_bg, a_cg, qkv, z, beta_raw, dec_raw, c_u, c_v, gates = split_cols(proj, SPLIT_SIZES)

    conv_a, new_hist_a = causal_dwconv(a_cg * a_h, hist_a, conv_a_w)
    out_a = (a_bg * conv_a) @ w_a_out

    qkv_c, new_hist_qkv = causal_dwconv(qkv, hist_qkv, conv_b_w)
    qkv_c = jax.nn.silu(qkv_c).astype(jnp.float32)
    q, k, v = jnp.split(qkv_c, 3, axis=-1)
    q = l2_normalize(q.reshape(Bn, L, H_B, DK)) * (DK ** -0.5)
    k = l2_normalize(k.reshape(Bn, L, H_B, DK))
    v = v.reshape(Bn, L, H_B, DV)
    beta = jax.nn.sigmoid(beta_raw.astype(jnp.float32))
    g = -jnp.exp(a_log.astype(jnp.float32)) * jax.nn.softplus(dec_raw.astype(jnp.float32) + dt_bias.astype(jnp.float32))
    o, new_s = gated_delta_rule(q, k, v, g, beta, s_delta.astype(jnp.float32))
    zf = z.astype(jnp.float32).reshape(Bn, L, H_B, DV)
    o = (rms_norm_f32(o, norm_b_g) * jax.nn.silu(zf)).astype(x.dtype)
    out_b = o.reshape(Bn, L, D_B) @ w_b_out

    u_c = jax.nn.gelu(c_u)
    v_c = layer_norm(jax.nn.gelu(c_v), ln_c_g, ln_c_b)
    out_c = (u_c * chunk_spatial_mix(v_c, w_s, b_s)) @ w_c_out

    gts = jax.nn.sigmoid(gates).reshape(Bn, L, N_BRANCH, D_MODEL)
    merged = gts[:, :, 0] * out_a + gts[:, :, 1] * out_b + gts[:, :, 2] * out_c
    x = layer_norm(DEEPNORM_ALPHA * x + merged @ w_o, ln1_g, ln1_b)

    hg, hu = jnp.split(x @ w_up, 2, axis=-1)
    hg_c, new_hist_f = causal_dwconv(hg, hist_f, conv_f_w)
    ffn = (jax.nn.silu(hg_c) * hu) @ w_down
    ple = jax.nn.sigmoid(x @ w_pg) * (p @ w_pe)
    x = layer_norm(DEEPNORM_ALPHA * x + ffn + ple, ln2_g, ln2_b)
    return x, new_hist_a, new_hist_qkv, new_s.astype(s_delta.dtype), new_hist_f, v_c


def setup_inputs(seed: int = 0) -> dict:
    key = jax.random.key(seed)
    ks = iter(jax.random.split(key, 48))
    f32 = jnp.float32

    def nrm(shape, scale):
        return jax.random.normal(next(ks), shape, f32) * scale

    x_prompt = nrm((BATCH, SEQ, D_MODEL), 1.0)
    x_sample = nrm((DEC_BATCH, DEC_SEQ, D_MODEL), 1.0)
    state_conv_a = nrm((DEPTH, DEC_BATCH, CONV_A - 1, D_A), 1.0)
    state_conv_qkv = nrm((DEPTH, DEC_BATCH, CONV_B - 1, 3 * D_B), 1.0)
    state_delta = nrm((DEPTH, DEC_BATCH, H_B, DK, DV), 0.5)
    state_conv_ffn = nrm((DEPTH, DEC_BATCH, CONV_F - 1, D_FF), 1.0)
    p_prompt = nrm((DEPTH, BATCH, SEQ, D_PLE), 1.0)
    p_sample = nrm((DEPTH, DEC_BATCH, DEC_SEQ, D_PLE), 1.0)
    ln_in_g = 1.0 + nrm((D_MODEL,), 0.02)
    ln_in_b = nrm((D_MODEL,), 0.02)
    w_in = nrm((DEPTH, D_MODEL, N_IN), D_MODEL ** -0.5)
    conv_a_w = nrm((DEPTH, CONV_A, D_A), CONV_A ** -0.5)
    w_a_out = nrm((DEPTH, D_A, D_MODEL), D_A ** -0.5)
    conv_b_w = nrm((DEPTH, CONV_B, 3 * D_B), CONV_B ** -0.5)
    a_log = jnp.log(jax.random.uniform(next(ks), (DEPTH, H_B), f32, 1.0, 16.0))
    dt = jnp.exp(jax.random.uniform(next(ks), (DEPTH, H_B), f32, math.log(1e-3), math.log(1e-1)))
    dt_bias = dt + jnp.log(-jnp.expm1(-dt))
    norm_b_g = 1.0 + nrm((DEPTH, DV), 0.02)
    w_b_out = nrm((DEPTH, D_B, D_MODEL), D_B ** -0.5)
    ln_c_g = 1.0 + nrm((DEPTH, D_C), 0.02)
    ln_c_b = nrm((DEPTH, D_C), 0.02)
    w_s = nrm((DEPTH, G_C, CHUNK_C, CHUNK_C), CHUNK_C ** -0.5)
    b_s = 1.0 + nrm((DEPTH, G_C, CHUNK_C), 0.1)
    w_c_out = nrm((DEPTH, D_C, D_MODEL), D_C ** -0.5)
    w_o = nrm((DEPTH, D_MODEL, D_MODEL), D_MODEL ** -0.5 * DEEPNORM_BETA)
    ln1_g = 1.0 + nrm((DEPTH, D_MODEL), 0.02)
    ln1_b = nrm((DEPTH, D_MODEL), 0.02)
    w_up = nrm((DEPTH, D_MODEL, 2 * D_FF), D_MODEL ** -0.5)
    conv_f_w = nrm((DEPTH, CONV_F, D_FF), CONV_F ** -0.5)
    w_down = nrm((DEPTH, D_FF, D_MODEL), D_FF ** -0.5 * DEEPNORM_BETA)
    w_pe = nrm((DEPTH, D_PLE, D_MODEL), D_PLE ** -0.5 * DEEPNORM_BETA)
    w_pg = nrm((DEPTH, D_MODEL, D_MODEL), D_MODEL ** -0.5)
    ln2_g = 1.0 + nrm((DEPTH, D_MODEL), 0.02)
    ln2_b = nrm((DEPTH, D_MODEL), 0.02)
    return {
        'x_prompt': x_prompt, 'x_sample': x_sample,
        'state_conv_a': state_conv_a, 'state_conv_qkv': state_conv_qkv,
        'state_delta': state_delta, 'state_conv_ffn': state_conv_ffn,
        'p_prompt': p_prompt, 'p_sample': p_sample,
        'ln_in_g': ln_in_g, 'ln_in_b': ln_in_b,
        'w_in': w_in, 'conv_a_w': conv_a_w, 'w_a_out': w_a_out,
        'conv_b_w': conv_b_w, 'a_log': a_log, 'dt_bias': dt_bias, 'norm_b_g': norm_b_g, 'w_b_out': w_b_out,
        'ln_c_g': ln_c_g, 'ln_c_b': ln_c_b, 'w_s': w_s, 'b_s': b_s, 'w_c_out': w_c_out,
        'w_o': w_o, 'ln1_g': ln1_g, 'ln1_b': ln1_b,
        'w_up': w_up, 'conv_f_w': conv_f_w, 'w_down': w_down,
        'w_pe': w_pe, 'w_pg': w_pg, 'ln2_g': ln2_g, 'ln2_b': ln2_b,
    }


def reference(x_prompt, x_sample, state_conv_a, state_conv_qkv, state_delta, state_conv_ffn,
              p_prompt, p_sample, ln_in_g, ln_in_b,
              w_in, conv_a_w, w_a_out, conv_b_w, a_log, dt_bias, norm_b_g, w_b_out,
              ln_c_g, ln_c_b, w_s, b_s, w_c_out, w_o, ln1_g, ln1_b,
              w_up, conv_f_w, w_down, w_pe, w_pg, ln2_g, ln2_b):
    xp = layer_norm(x_prompt, ln_in_g, ln_in_b)
    xs = layer_norm(x_sample, ln_in_g, ln_in_b)
    bp = x_prompt.shape[0]
    dt_p = x_prompt.dtype
    pa, pq, pd, pf = [], [], [], []
    sa, sq, sd, sf, sv = [], [], [], [], []
    for i in range(DEPTH):
        wts = (w_in[i], conv_a_w[i], w_a_out[i], conv_b_w[i], a_log[i], dt_bias[i], norm_b_g[i], w_b_out[i],
               ln_c_g[i], ln_c_b[i], w_s[i], b_s[i], w_c_out[i], w_o[i], ln1_g[i], ln1_b[i],
               w_up[i], conv_f_w[i], w_down[i], w_pe[i], w_pg[i], ln2_g[i], ln2_b[i])
        xp, ha, hq, hs, hf, _ = layer_step(
            xp, p_prompt[i],
            jnp.zeros((bp, CONV_A - 1, D_A), dt_p),
            jnp.zeros((bp, CONV_B - 1, 3 * D_B), dt_p),
            jnp.zeros((bp, H_B, DK, DV), dt_p),
            jnp.zeros((bp, CONV_F - 1, D_FF), dt_p),
            *wts)
        pa.append(ha); pq.append(hq); pd.append(hs); pf.append(hf)
        xs, ha, hq, hs, hf, vrows = layer_step(
            xs, p_sample[i], state_conv_a[i], state_conv_qkv[i], state_delta[i], state_conv_ffn[i], *wts)
        sa.append(ha); sq.append(hq); sd.append(hs); sf.append(hf); sv.append(vrows)
    new_conv_a_p = jnp.stack(pa)
    new_conv_qkv_p = jnp.stack(pq)
    new_delta_p = jnp.stack(pd)
    new_conv_ffn_p = jnp.stack(pf)
    new_conv_a_s = jnp.stack(sa)
    new_conv_qkv_s = jnp.stack(sq)
    new_delta_s = jnp.stack(sd)
    new_conv_ffn_s = jnp.stack(sf)
    new_vchunk_s = jnp.stack(sv)
    return (xp, xs, new_conv_a_p, new_conv_qkv_p, new_delta_p, new_conv_ffn_p,
            new_conv_a_s, new_conv_qkv_s, new_delta_s, new_conv_ffn_s, new_vchunk_s)
```

```python
import functools
import math

import jax
import jax.numpy as jnp
from jax import lax
from jax.experimental import pallas as pl
from jax.experimental.pallas import tpu as pltpu

F32 = jnp.float32
BF16 = jnp.bfloat16

D_MODEL = 2048
DEPTH = 2
D_PLE = 256
D_A = D_MODEL // 2
CONV_A = 3
H_B = 8
DK = 128
DV = 128
D_B = H_B * DV
CONV_B = 4
DELTA_CHUNK = 64
D_C = D_MODEL // 2
CHUNK_C = 128
G_C = 8
D_FF = ((8 * D_MODEL // 3 + 127) // 128) * 128
CONV_F = 3
N_BRANCH = 3
DEEPNORM_ALPHA = (2 * DEPTH) ** 0.25
LN_EPS = 1e-5
RMS_EPS = 1e-6

LANES = 128
SUBLANES = 8
HIST_ROWS = SUBLANES
D_FF_PAD = 5632
VMEM_LIMIT = 48 * 1024 * 1024

OFF_AH, OFF_ABG, OFF_ACG = 0, 1024, 2048
OFF_Q, OFF_K, OFF_V = 3072, 4096, 5120
OFF_Z, OFF_CU, OFF_CV = 6144, 7168, 8192
OFF_G0, OFF_G1, OFF_G2 = 9216, 11264, 13312
N_MAIN = 15360


def _params(*sem):
    return pltpu.CompilerParams(dimension_semantics=sem, vmem_limit_bytes=VMEM_LIMIT)


def _sigmoid(x):
    return 1.0 / (1.0 + jnp.exp(-x))


def _silu(x):
    return x * _sigmoid(x)


def _gelu(x):
    c = math.sqrt(2.0 / math.pi)
    return 0.5 * x * (1.0 + jnp.tanh(c * (x + 0.044715 * (x * x * x))))


def _softplus(x):
    return jnp.maximum(x, 0.0) + jnp.log1p(jnp.exp(-jnp.abs(x)))


def _ln(x, g, b):
    mu = jnp.mean(x, axis=-1, keepdims=True)
    xc = x - mu
    var = jnp.mean(xc * xc, axis=-1, keepdims=True)
    return xc * lax.rsqrt(var + LN_EPS) * g + b


def _dot(a, b):
    return jnp.dot(a.astype(BF16), b.astype(BF16), preferred_element_type=F32)


def _dot_nt(a, b):
    return lax.dot_general(a.astype(BF16), b.astype(BF16), (((1,), (1,)), ((), ())),
                           preferred_element_type=F32)


def _dot_tn(a, b):
    return lax.dot_general(a.astype(BF16), b.astype(BF16), (((0,), (0,)), ((), ())),
                           preferred_element_type=F32)


def _dot_hi(a, b):
    return jnp.dot(a, b, precision=lax.Precision.HIGHEST, preferred_element_type=F32)


def _mm_kernel(x_ref, w_ref, o_ref):
    o_ref[...] = _dot(x_ref[...], w_ref[...]).astype(o_ref.dtype)


def _mm(x, w, bm, bn, out_dtype, name):
    m, k = x.shape
    n = w.shape[1]
    assert m % bm == 0 and n % bn == 0
    return pl.pallas_call(
        _mm_kernel,
        out_shape=jax.ShapeDtypeStruct((m, n), out_dtype),
        grid=(m // bm, n // bn),
        in_specs=[pl.BlockSpec((bm, k), lambda i, j: (i, 0)),
                  pl.BlockSpec((k, bn), lambda i, j: (0, j))],
        out_specs=pl.BlockSpec((bm, bn), lambda i, j: (i, j)),
        compiler_params=_params("parallel", "arbitrary"),
        name=name,
    )(x, w)


def _ln_sum_kernel(*refs, scales):
    n = len(scales)
    g_ref, b_ref, o_ref = refs[n], refs[n + 1], refs[n + 2]
    acc = None
    for r, s in zip(refs[:n], scales):
        v = r[...]
        if s != 1.0:
            v = s * v
        acc = v if acc is None else acc + v
    o_ref[...] = _ln(acc, g_ref[...], b_ref[...])


def _ln_sum(terms, g, b, bm, name):
    arrs = [t[0] for t in terms]
    scales = tuple(float(t[1]) for t in terms)
    m, d = arrs[0].shape
    row_spec = pl.BlockSpec((bm, d), lambda i: (i, 0))
    vec_spec = pl.BlockSpec((1, d), lambda i: (0, 0))
    return pl.pallas_call(
        functools.partial(_ln_sum_kernel, scales=scales),
        out_shape=jax.ShapeDtypeStruct((m, d), F32),
        grid=(m // bm,),
        in_specs=[row_spec] * len(arrs) + [vec_spec, vec_spec],
        out_specs=row_spec,
        compiler_params=_params("parallel"),
        name=name,
    )(*arrs, g.reshape(1, d), b.reshape(1, d))


def _conv_rows(buf_ref, x, hist, w, width):
    rows = x.shape[0]
    buf_ref[0:HIST_ROWS, :] = hist
    buf_ref[HIST_ROWS:HIST_ROWS + rows, :] = x
    out = None
    for j in range(width):
        start = HIST_ROWS - (width - 1) + j
        term = buf_ref[start:start + rows, :] * w[j:j + 1, :]
        out = term if out is None else out + term
    return out


def _conv_time_major(xs, w, width):
    n_out = len(xs) - (width - 1)
    outs = []
    for t in range(n_out):
        out = None
        for j in range(width):
            term = xs[t + j] * w[j:j + 1, :]
            out = term if out is None else out + term
        outs.append(out)
    return outs


def _mixa_p_kernel(h_ref, bg_ref, cg_ref, hist_ref, w_ref, y_ref, tail_ref, buf_ref):
    u = cg_ref[...] * h_ref[...]
    rows = u.shape[0]
    conv = _conv_rows(buf_ref, u, hist_ref[...], w_ref[...], CONV_A)
    y_ref[...] = (bg_ref[...] * conv).astype(y_ref.dtype)
    tail_ref[...] = u[rows - HIST_ROWS:, :]


def _mixa_p(proj3, hist, w8, cb=256):
    nb, seq, _ = proj3.shape
    nc = D_A // cb

    def col(off):
        return pl.BlockSpec((None, seq, cb), lambda b, j, o=off // cb: (b, 0, o + j))

    return pl.pallas_call(
        _mixa_p_kernel,
        out_shape=(jax.ShapeDtypeStruct((nb, seq, D_A), BF16),
                   jax.ShapeDtypeStruct((nb, HIST_ROWS, D_A), F32)),
        grid=(nb, nc),
        in_specs=[col(OFF_AH), col(OFF_ABG), col(OFF_ACG),
                  pl.BlockSpec((None, HIST_ROWS, cb), lambda b, j: (b, 0, j)),
                  pl.BlockSpec((HIST_ROWS, cb), lambda b, j: (0, j))],
        out_specs=(pl.BlockSpec((None, seq, cb), lambda b, j: (b, 0, j)),
                   pl.BlockSpec((None, HIST_ROWS, cb), lambda b, j: (b, 0, j))),
        scratch_shapes=[pltpu.VMEM((seq + HIST_ROWS, cb), F32)],
        compiler_params=_params("parallel", "parallel"),
        name="mixa_prompt",
    )(proj3, proj3, proj3, hist, w8)


def _mixa_s_kernel(h_ref, bg_ref, cg_ref, hist_ref, w_ref, y_ref, u_ref):
    steps = h_ref.shape[0]
    us = []
    for t in range(steps):
        u = cg_ref[t] * h_ref[t]
        u_ref[t] = u
        us.append(u)
    xs = [hist_ref[t] for t in range(CONV_A - 1)] + us
    convs = _conv_time_major(xs, w_ref[...], CONV_A)
    for t in range(steps):
        y_ref[t] = (bg_ref[t] * convs[t]).astype(y_ref.dtype)


def _mixa_s(proj4, hist_tm, w8, cb=512):
    steps, nb, _ = proj4.shape
    nc = D_A // cb

    def col(off):
        return pl.BlockSpec((steps, nb, cb), lambda j, o=off // cb: (0, 0, o + j))

    return pl.pallas_call(
        _mixa_s_kernel,
        out_shape=(jax.ShapeDtypeStruct((steps, nb, D_A), BF16),
                   jax.ShapeDtypeStruct((steps, nb, D_A), F32)),
        grid=(nc,),
        in_specs=[col(OFF_AH), col(OFF_ABG), col(OFF_ACG),
                  pl.BlockSpec((CONV_A - 1, nb, cb), lambda j: (0, 0, j)),
                  pl.BlockSpec((HIST_ROWS, cb), lambda j: (0, j))],
        out_specs=(pl.BlockSpec((steps, nb, cb), lambda j: (0, 0, j)),
                   pl.BlockSpec((steps, nb, cb), lambda j: (0, 0, j))),
        compiler_params=_params("parallel"),
        name="mixa_sample",
    )(proj4, proj4, proj4, hist_tm, w8)


def _l2n(t):
    return t * lax.rsqrt(jnp.sum(t * t, axis=-1, keepdims=True) + RMS_EPS)


def _qkv_p_kernel(q_ref, k_ref, v_ref, hq_ref, hk_ref, hv_ref, wq_ref, wk_ref, wv_ref,
                  qo_ref, ko_ref, vo_ref, buf_ref):
    q = _silu(_conv_rows(buf_ref, q_ref[...], hq_ref[...], wq_ref[...], CONV_B))
    qo_ref[...] = _l2n(q) * (DK ** -0.5)
    k = _silu(_conv_rows(buf_ref, k_ref[...], hk_ref[...], wk_ref[...], CONV_B))
    ko_ref[...] = _l2n(k)
    vo_ref[...] = _silu(_conv_rows(buf_ref, v_ref[...], hv_ref[...], wv_ref[...], CONV_B))


def _qkv_p(proj3, hist, w8):
    nb, seq, _ = proj3.shape
    cb = DK
    nh = D_B // cb

    def col(off):
        return pl.BlockSpec((None, seq, cb), lambda b, j, o=off // cb: (b, 0, o + j))

    def hcol(part):
        return pl.BlockSpec((None, HIST_ROWS, cb), lambda b, j, o=part * nh: (b, 0, o + j))

    def wcol(part):
        return pl.BlockSpec((HIST_ROWS, cb), lambda b, j, o=part * nh: (0, o + j))

    out = jax.ShapeDtypeStruct((nb, seq, D_B), F32)
    ospec = pl.BlockSpec((None, seq, cb), lambda b, j: (b, 0, j))
    return pl.pallas_call(
        _qkv_p_kernel,
        out_shape=(out, out, out),
        grid=(nb, nh),
        in_specs=[col(OFF_Q), col(OFF_K), col(OFF_V), hcol(0), hcol(1), hcol(2),
                  wcol(0), wcol(1), wcol(2)],
        out_specs=(ospec, ospec, ospec),
        scratch_shapes=[pltpu.VMEM((seq + HIST_ROWS, cb), F32)],
        compiler_params=_params("parallel", "parallel"),
        name="qkv_prompt",
    )(proj3, proj3, proj3, hist, hist, hist, w8, w8, w8)


def _qkv_s_kernel(q_ref, k_ref, v_ref, hq_ref, hk_ref, hv_ref, wq_ref, wk_ref, wv_ref,
                  qo_ref, ko_ref, vo_ref):
    steps = q_ref.shape[0]

    def conv(x_ref, h_ref, w_ref):
        xs = [h_ref[t] for t in range(CONV_B - 1)] + [x_ref[t] for t in range(steps)]
        return _conv_time_major(xs, w_ref[...], CONV_B)

    for t, c in enumerate(conv(q_ref, hq_ref, wq_ref)):
        qo_ref[t] = _l2n(_silu(c)) * (DK ** -0.5)
    for t, c in enumerate(conv(k_ref, hk_ref, wk_ref)):
        ko_ref[t] = _l2n(_silu(c))
    for t, c in enumerate(conv(v_ref, hv_ref, wv_ref)):
        vo_ref[t] = _silu(c)


def _qkv_s(proj4, hist_tm, w8):
    steps, nb, _ = proj4.shape
    cb = DK
    nh = D_B // cb

    def col(off):
        return pl.BlockSpec((steps, nb, cb), lambda j, o=off // cb: (0, 0, o + j))

    def hcol(part):
        return pl.BlockSpec((CONV_B - 1, nb, cb), lambda j, o=part * nh: (0, 0, o + j))

    def wcol(part):
        return pl.BlockSpec((HIST_ROWS, cb), lambda j, o=part * nh: (0, o + j))

    out = jax.ShapeDtypeStruct((steps, nb, D_B), F32)
    ospec = pl.BlockSpec((steps, nb, cb), lambda j: (0, 0, j))
    return pl.pallas_call(
        _qkv_s_kernel,
        out_shape=(out, out, out),
        grid=(nh,),
        in_specs=[col(OFF_Q), col(OFF_K), col(OFF_V), hcol(0), hcol(1), hcol(2),
                  wcol(0), wcol(1), wcol(2)],
        out_specs=(ospec, ospec, ospec),
        compiler_params=_params("parallel"),
        name="qkv_sample",
    )(proj4, proj4, proj4, hist_tm, hist_tm, hist_tm, w8, w8, w8)


def _neumann(x, eye, order):
    p = eye - x
    xp = x
    n = 2
    while n < order:
        xp = _dot_hi(xp, xp)
        p = p + _dot_hi(p, xp)
        n *= 2
    return p


def _unit_lower_inverse(lmat, chunk, blk):
    row = lax.broadcasted_iota(jnp.int32, (chunk, chunk), 0)
    col = lax.broadcasted_iota(jnp.int32, (chunk, chunk), 1)
    eye = (row == col).astype(F32)
    same = jnp.bitwise_xor(row, col) < blk
    d_inv = _neumann(jnp.where(same, lmat, 0.0), eye, blk)
    m = _dot_hi(d_inv, jnp.where(same, 0.0, lmat))
    return _dot_hi(_neumann(m, eye, chunk // blk), d_inv)


def _delta_kernel(q_ref, k_ref, v_ref, bg_ref, z_ref, alog_ref, dtb_ref, ng_ref, s0_ref,
                  y_ref, sfin_ref, s_scr, *, chunk, valid, n_chunks, blk):
    c = pl.program_id(1)

    @pl.when(c == 0)
    def _():
        s_scr[...] = s0_ref[...]

    q = q_ref[...]
    k = k_ref[...]
    v = v_ref[...]
    z = z_ref[...]
    beta = _sigmoid(bg_ref[:, 0:LANES])
    g = -jnp.exp(alog_ref[...]) * _softplus(bg_ref[:, LANES:2 * LANES] + dtb_ref[...])
    if valid < chunk:
        live = lax.broadcasted_iota(jnp.int32, (chunk, LANES), 0) < valid
        beta = jnp.where(live, beta, 0.0)
        g = jnp.where(live, g, 0.0)

    row = lax.broadcasted_iota(jnp.int32, (chunk, chunk), 0)
    col = lax.broadcasted_iota(jnp.int32, (chunk, chunk), 1)
    causal = row >= col
    strict = row > col
    gcum = _dot_hi(causal.astype(F32), g)
    gpad = jnp.concatenate([gcum, jnp.zeros((LANES - chunk, LANES), F32)], axis=0)
    gcum_t = gpad.T
    ng = ng_ref[...]

    for h in range(H_B):
        sl = slice(h * DK, (h + 1) * DK)
        qh, kh, vh, zh = q[:, sl], k[:, sl], v[:, sl], z[:, sl]
        bcol = beta[:, h:h + 1]
        gcol = gcum[:, h:h + 1]
        grow = gcum_t[h:h + 1, 0:chunk]
        decay = jnp.exp(jnp.where(causal, gcol - grow, -jnp.inf))
        kb = kh * bcol
        vb = vh * bcol
        lmat = jnp.where(strict, _dot_nt(kb, kh) * decay, 0.0)
        tmat = _unit_lower_inverse(lmat, chunk, blk)
        eg = jnp.exp(gcol)
        u = _dot(tmat, vb)
        w = _dot(tmat, kb * eg)
        qk = _dot_nt(qh, kh) * decay
        glast = gcum[chunk - 1:chunk, h:h + 1]
        s = s_scr[h]
        v_new = u - _dot(w, s)
        o = _dot(qh * eg, s) + _dot(qk, v_new)
        s_scr[h] = s * jnp.exp(glast) + _dot_tn(kh * jnp.exp(glast - gcol), v_new)
        on = o * lax.rsqrt(jnp.mean(o * o, axis=-1, keepdims=True) + RMS_EPS) * ng
        y_ref[:, sl] = (on * _silu(zh)).astype(y_ref.dtype)

    @pl.when(c == n_chunks - 1)
    def _():
        sfin_ref[...] = s_scr[...]


def _delta(q, k, v, bg, z3, zcol, alog, dtb, ng, s0, chunk, valid, blk, name):
    nb, seq, _ = q.shape
    n_chunks = seq // chunk
    tok = pl.BlockSpec((None, chunk, D_B), lambda b, c: (b, c, 0))
    vec = pl.BlockSpec((1, LANES), lambda b, c: (0, 0))
    st = pl.BlockSpec((None, H_B, DK, DV), lambda b, c: (b, 0, 0, 0))
    return pl.pallas_call(
        functools.partial(_delta_kernel, chunk=chunk, valid=valid, n_chunks=n_chunks, blk=blk),
        out_shape=(jax.ShapeDtypeStruct((nb, seq, D_B), BF16),
                   jax.ShapeDtypeStruct((nb, H_B, DK, DV), F32)),
        grid=(nb, n_chunks),
        in_specs=[tok, tok, tok,
                  pl.BlockSpec((None, chunk, 2 * LANES), lambda b, c: (b, c, 0)),
                  pl.BlockSpec((None, chunk, D_B), lambda b, c, o=zcol: (b, c, o)),
                  vec, vec, vec, st],
        out_specs=(tok, st),
        scratch_shapes=[pltpu.VMEM((H_B, DK, DV), F32)],
        compiler_params=_params("parallel", "arbitrary"),
        name=name,
    )(q, k, v, bg, z3, alog, dtb, ng, s0)


def _mixc_p_kernel(cu_ref, cv_ref, g_ref, b_ref, ws_ref, bias_ref, y_ref):
    vc = _ln(_gelu(cv_ref[...]), g_ref[...], b_ref[...])
    u = _gelu(cu_ref[...])
    row = lax.broadcasted_iota(jnp.int32, (CHUNK_C, CHUNK_C), 0)
    col = lax.broadcasted_iota(jnp.int32, (CHUNK_C, CHUNK_C), 1)
    tri = row >= col
    cg = D_C // G_C
    for g in range(G_C):
        sl = slice(g * cg, (g + 1) * cg)
        wm = jnp.where(tri, ws_ref[g], 0.0)
        mixed = _dot(wm, vc[:, sl]) + bias_ref[:, sl]
        y_ref[:, sl] = (u[:, sl] * mixed).astype(y_ref.dtype)


def _mixc_p(proj3, g, b, ws, bias):
    nb, seq, _ = proj3.shape
    vec = pl.BlockSpec((1, D_C), lambda bb, n: (0, 0))
    return pl.pallas_call(
        _mixc_p_kernel,
        out_shape=jax.ShapeDtypeStruct((nb, seq, D_C), BF16),
        grid=(nb, seq // CHUNK_C),
        in_specs=[pl.BlockSpec((None, CHUNK_C, D_C), lambda bb, n: (bb, n, OFF_CU // D_C)),
                  pl.BlockSpec((None, CHUNK_C, D_C), lambda bb, n: (bb, n, OFF_CV // D_C)),
                  vec, vec,
                  pl.BlockSpec((G_C, CHUNK_C, CHUNK_C), lambda bb, n: (0, 0, 0)),
                  pl.BlockSpec((CHUNK_C, D_C), lambda bb, n: (0, 0))],
        out_specs=pl.BlockSpec((None, CHUNK_C, D_C), lambda bb, n: (bb, n, 0)),
        compiler_params=_params("parallel", "parallel"),
        name="mixc_prompt",
    )(proj3, proj3, g, b, ws, bias)


def _mixc_s_kernel(cu_ref, cv_ref, g_ref, b_ref, wc_ref, bias_ref, y_ref, vc_ref):
    steps = cu_ref.shape[0]
    vcs = []
    for t in range(steps):
        vc = _ln(_gelu(cv_ref[t]), g_ref[...], b_ref[...])
        vc_ref[t] = vc
        vcs.append(vc)
    for t in range(steps):
        acc = None
        for s in range(t + 1):
            term = wc_ref[t * steps + s:t * steps + s + 1, :] * vcs[s]
            acc = term if acc is None else acc + term
        mixed = acc + bias_ref[t:t + 1, :]
        y_ref[t] = (_gelu(cu_ref[t]) * mixed).astype(y_ref.dtype)


def _mixc_s(proj4, g, b, wc, bias):
    steps, nb, _ = proj4.shape
    full = lambda shape: pl.BlockSpec(shape, lambda i: (0,) * len(shape))
    return pl.pallas_call(
        _mixc_s_kernel,
        out_shape=(jax.ShapeDtypeStruct((steps, nb, D_C), BF16),
                   jax.ShapeDtypeStruct((steps, nb, D_C), F32)),
        grid=(1,),
        in_specs=[pl.BlockSpec((steps, nb, D_C), lambda i: (0, 0, OFF_CU // D_C)),
                  pl.BlockSpec((steps, nb, D_C), lambda i: (0, 0, OFF_CV // D_C)),
                  full((1, D_C)), full((1, D_C)), full(wc.shape), full(bias.shape)],
        out_specs=(full((steps, nb, D_C)), full((steps, nb, D_C))),
        compiler_params=_params("arbitrary"),
        name="mixc_sample",
    )(proj4, proj4, g, b, wc, bias)


def _merge_kernel(ya_ref, yb_ref, yc_ref, wa_ref, wb_ref, wc_ref, g0_ref, g1_ref, g2_ref, o_ref):
    m = _sigmoid(g0_ref[...]) * _dot(ya_ref[...], wa_ref[...])
    m = m + _sigmoid(g1_ref[...]) * _dot(yb_ref[...], wb_ref[...])
    m = m + _sigmoid(g2_ref[...]) * _dot(yc_ref[...], wc_ref[...])
    o_ref[...] = m.astype(o_ref.dtype)


def _merge(ya, yb, yc, wa, wb, wc, proj, bm=512, bn=512):
    m = ya.shape[0]
    xs = pl.BlockSpec((bm, D_A), lambda i, j: (i, 0))
    ws = pl.BlockSpec((D_A, bn), lambda i, j: (0, j))

    def gate(off):
        return pl.BlockSpec((bm, bn), lambda i, j, o=off // bn: (i, o + j))

    return pl.pallas_call(
        _merge_kernel,
        out_shape=jax.ShapeDtypeStruct((m, D_MODEL), BF16),
        grid=(m // bm, D_MODEL // bn),
        in_specs=[xs, xs, xs, ws, ws, ws, gate(OFF_G0), gate(OFF_G1), gate(OFF_G2)],
        out_specs=pl.BlockSpec((bm, bn), lambda i, j: (i, j)),
        compiler_params=_params("parallel", "arbitrary"),
        name="merge",
    )(ya, yb, yc, wa, wb, wc, proj, proj, proj)


def _ffn_p_kernel(hg_ref, hu_ref, hist_ref, w_ref, o_ref, buf_ref):
    conv = _conv_rows(buf_ref, hg_ref[...], hist_ref[...], w_ref[...], CONV_F)
    o_ref[...] = (_silu(conv) * hu_ref[...]).astype(o_ref.dtype)


def _ffn_p(h3, hist, w8, cb=256):
    nb, seq, _ = h3.shape
    nc = D_FF_PAD // cb
    return pl.pallas_call(
        _ffn_p_kernel,
        out_shape=jax.ShapeDtypeStruct((nb, seq, D_FF_PAD), BF16),
        grid=(nb, nc),
        in_specs=[pl.BlockSpec((None, seq, cb), lambda b, j: (b, 0, j)),
                  pl.BlockSpec((None, seq, cb), lambda b, j: (b, 0, nc + j)),
                  pl.BlockSpec((None, HIST_ROWS, cb), lambda b, j: (b, 0, j)),
                  pl.BlockSpec((HIST_ROWS, cb), lambda b, j: (0, j))],
        out_specs=pl.BlockSpec((None, seq, cb), lambda b, j: (b, 0, j)),
        scratch_shapes=[pltpu.VMEM((seq + HIST_ROWS, cb), F32)],
        compiler_params=_params("parallel", "parallel"),
        name="ffnact_prompt",
    )(h3, h3, hist, w8)


def _ffn_s_kernel(hg_ref, hu_ref, hist_ref, w_ref, o_ref):
    steps = hg_ref.shape[0]
    xs = [hist_ref[t] for t in range(CONV_F - 1)] + [hg_ref[t] for t in range(steps)]
    for t, c in enumerate(_conv_time_major(xs, w_ref[...], CONV_F)):
        o_ref[t] = (_silu(c) * hu_ref[t]).astype(o_ref.dtype)


def _ffn_s(h4, hist_tm, w8, cb=512):
    steps, nb, _ = h4.shape
    nc = D_FF_PAD // cb
    return pl.pallas_call(
        _ffn_s_kernel,
        out_shape=jax.ShapeDtypeStruct((steps, nb, D_FF_PAD), BF16),
        grid=(nc,),
        in_specs=[pl.BlockSpec((steps, nb, cb), lambda j: (0, 0, j)),
                  pl.BlockSpec((steps, nb, cb), lambda j: (0, 0, nc + j)),
                  pl.BlockSpec((CONV_F - 1, nb, cb), lambda j: (0, 0, j)),
                  pl.BlockSpec((HIST_ROWS, cb), lambda j: (0, j))],
        out_specs=pl.BlockSpec((steps, nb, cb), lambda j: (0, 0, j)),
        compiler_params=_params("parallel"),
        name="ffnact_sample",
    )(h4, h4, hist_tm, w8)


def _ple_kernel(x_ref, wpg_ref, p_ref, wpe_ref, o_ref):
    o_ref[...] = _sigmoid(_dot(x_ref[...], wpg_ref[...])) * _dot(p_ref[...], wpe_ref[...])


def _ple(x, wpg, p, wpe, bm=512, bn=512):
    m = x.shape[0]
    return pl.pallas_call(
        _ple_kernel,
        out_shape=jax.ShapeDtypeStruct((m, D_MODEL), F32),
        grid=(m // bm, D_MODEL // bn),
        in_specs=[pl.BlockSpec((bm, D_MODEL), lambda i, j: (i, 0)),
                  pl.BlockSpec((D_MODEL, bn), lambda i, j: (0, j)),
                  pl.BlockSpec((bm, D_PLE), lambda i, j: (i, 0)),
                  pl.BlockSpec((D_PLE, bn), lambda i, j: (0, j))],
        out_specs=pl.BlockSpec((bm, bn), lambda i, j: (i, j)),
        compiler_params=_params("parallel", "arbitrary"),
        name="ple",
    )(x, wpg, p, wpe)


def _pad_rows(w, rows):
    return jnp.pad(w, ((0, rows - w.shape[0]), (0, 0)))


def _pad_cols(w, cols):
    return jnp.pad(w, ((0, 0),) * (w.ndim - 1) + ((0, cols - w.shape[-1]),))


def _layer_weights(i, w_in, conv_a_w, w_a_out, conv_b_w, a_log, dt_bias, norm_b_g, w_b_out,
                   ln_c_g, ln_c_b, w_s, b_s, w_c_out, w_o, ln1_g, ln1_b,
                   w_up, conv_f_w, w_down, w_pe, w_pg, ln2_g, ln2_b):
    off_beta = 3 * D_A + 3 * D_B + D_B
    off_dec = off_beta + H_B
    off_cu = off_dec + H_B
    wi = w_in[i]
    w = {}
    w["main"] = jnp.concatenate([wi[:, :off_beta], wi[:, off_cu:]], axis=1).astype(BF16)
    w["small"] = jnp.concatenate([_pad_cols(wi[:, off_beta:off_dec], LANES),
                                  _pad_cols(wi[:, off_dec:off_cu], LANES)], axis=1).astype(BF16)
    w["conv_a"] = _pad_rows(conv_a_w[i], HIST_ROWS)
    w["conv_b"] = _pad_rows(conv_b_w[i], HIST_ROWS)
    w["conv_f"] = _pad_cols(_pad_rows(conv_f_w[i], HIST_ROWS), D_FF_PAD)
    w["a_out"] = w_a_out[i].astype(BF16)
    w["b_out"] = w_b_out[i].astype(BF16)
    w["c_out"] = w_c_out[i].astype(BF16)
    w["alog"] = _pad_cols(a_log[i].reshape(1, H_B), LANES)
    w["dtb"] = _pad_cols(dt_bias[i].reshape(1, H_B), LANES)
    w["ng"] = norm_b_g[i].reshape(1, DV)
    w["lncg"] = ln_c_g[i].reshape(1, D_C)
    w["lncb"] = ln_c_b[i].reshape(1, D_C)
    w["ws"] = w_s[i]
    cg = D_C // G_C
    w["bias_c"] = jnp.repeat(b_s[i].T, cg, axis=1)
    w["o"] = w_o[i].astype(BF16)
    w["ln1g"], w["ln1b"] = ln1_g[i], ln1_b[i]
    wu = w_up[i]
    w["up"] = jnp.concatenate([_pad_cols(wu[:, :D_FF], D_FF_PAD),
                               _pad_cols(wu[:, D_FF:], D_FF_PAD)], axis=1).astype(BF16)
    w["down"] = _pad_rows(w_down[i], D_FF_PAD).astype(BF16)
    w["pe"] = w_pe[i].astype(BF16)
    w["pg"] = w_pg[i].astype(BF16)
    w["ln2g"], w["ln2b"] = ln2_g[i], ln2_b[i]
    return w


def _sample_mix_coeffs(w, steps):
    cg = D_C // G_C
    wc = jnp.transpose(w["ws"][:, :steps, :steps], (1, 2, 0))
    wc = jnp.repeat(wc, cg, axis=2).reshape(steps * steps, D_C)
    return wc, w["bias_c"][:HIST_ROWS]


def _layer_prompt(x, p2d, w, nb, seq):
    m = x.shape[0]
    proj = _mm(x, w["main"], 1024, 512, F32, "in_proj")
    small = _mm(x, w["small"], 1024, 2 * LANES, F32, "in_proj_small")
    proj3 = proj.reshape(nb, seq, N_MAIN)

    ya, tail_a = _mixa_p(proj3, jnp.zeros((nb, HIST_ROWS, D_A), F32), w["conv_a"])
    q, k, v = _qkv_p(proj3, jnp.zeros((nb, HIST_ROWS, 3 * D_B), F32), w["conv_b"])
    yb, s_fin = _delta(q, k, v, small.reshape(nb, seq, 2 * LANES), proj3, OFF_Z // D_B,
                       w["alog"], w["dtb"], w["ng"], jnp.zeros((nb, H_B, DK, DV), F32),
                       DELTA_CHUNK, DELTA_CHUNK, SUBLANES, "delta_prompt")
    yc = _mixc_p(proj3, w["lncg"], w["lncb"], w["ws"], w["bias_c"])

    merged = _merge(ya.reshape(m, D_A), yb.reshape(m, D_B), yc.reshape(m, D_C),
                    w["a_out"], w["b_out"], w["c_out"], proj)
    mo = _mm(merged, w["o"], 1024, 512, F32, "o_proj")
    x1 = _ln_sum([(x, DEEPNORM_ALPHA), (mo, 1.0)], w["ln1g"], w["ln1b"], 256, "ln1")

    h = _mm(x1, w["up"], 1024, 512, F32, "up_proj")
    h3 = h.reshape(nb, seq, 2 * D_FF_PAD)
    act = _ffn_p(h3, jnp.zeros((nb, HIST_ROWS, D_FF_PAD), F32), w["conv_f"])
    ffn = _mm(act.reshape(m, D_FF_PAD), w["down"], 512, 512, F32, "down_proj")
    ple = _ple(x1, w["pg"], p2d, w["pe"])
    x2 = _ln_sum([(x1, DEEPNORM_ALPHA), (ffn, 1.0), (ple, 1.0)], w["ln2g"], w["ln2b"], 256, "ln2")

    new_a = tail_a[:, HIST_ROWS - (CONV_A - 1):, :]
    new_q = proj3[:, seq - (CONV_B - 1):, OFF_Q:OFF_Q + 3 * D_B]
    new_f = h3[:, seq - (CONV_F - 1):, :D_FF]
    return x2, new_a, new_q, s_fin, new_f


def _tm(a):
    return jnp.swapaxes(a, 0, 1)


def _layer_sample(x, p2d, w, st_a, st_q, st_d, st_f, nb, steps):
    m = x.shape[0]
    chunk = 2 * SUBLANES
    proj = _mm(x, w["main"], m, 512, F32, "in_proj_s")
    small = _mm(x, w["small"], m, 2 * LANES, F32, "in_proj_small_s")
    proj4 = proj.reshape(steps, nb, N_MAIN)

    ya, u = _mixa_s(proj4, _tm(st_a), w["conv_a"])
    q, k, v = _qkv_s(proj4, _tm(st_q), w["conv_b"])

    def to_chunks(a_tm):
        return jnp.pad(_tm(a_tm), ((0, 0), (0, chunk - steps), (0, 0)))

    yb_b, s_fin = _delta(to_chunks(q), to_chunks(k), to_chunks(v),
                         to_chunks(small.reshape(steps, nb, 2 * LANES)),
                         to_chunks(proj4[:, :, OFF_Z:OFF_Z + D_B]), 0,
                         w["alog"], w["dtb"], w["ng"], st_d,
                         chunk, steps, SUBLANES // 2, "delta_sample")
    yb = _tm(yb_b[:, :steps]).reshape(m, D_B)

    wc, bias = _sample_mix_coeffs(w, steps)
    yc, vc = _mixc_s(proj4, w["lncg"], w["lncb"], wc, bias)

    merged = _merge(ya.reshape(m, D_A), yb, yc.reshape(m, D_C),
                    w["a_out"], w["b_out"], w["c_out"], proj, bm=m)
    mo = _mm(merged, w["o"], m, 512, F32, "o_proj_s")
    x1 = _ln_sum([(x, DEEPNORM_ALPHA), (mo, 1.0)], w["ln1g"], w["ln1b"], 256, "ln1_s")

    h = _mm(x1, w["up"], m, 512, F32, "up_proj_s")
    h4 = h.reshape(steps, nb, 2 * D_FF_PAD)
    act = _ffn_s(h4, _pad_cols(_tm(st_f), D_FF_PAD), w["conv_f"])
    ffn = _mm(act.reshape(m, D_FF_PAD), w["down"], m, 512, F32, "down_proj_s")
    ple = _ple(x1, w["pg"], p2d, w["pe"], bm=m)
    x2 = _ln_sum([(x1, DEEPNORM_ALPHA), (ffn, 1.0), (ple, 1.0)], w["ln2g"], w["ln2b"], 256, "ln2_s")

    new_a = _tm(u[steps - (CONV_A - 1):])
    new_q = _tm(proj4[steps - (CONV_B - 1):, :, OFF_Q:OFF_Q + 3 * D_B])
    new_f = _tm(h4[steps - (CONV_F - 1):, :, :D_FF])
    return x2, new_a, new_q, s_fin, new_f, _tm(vc)


def kernel(x_prompt, x_sample, state_conv_a, state_conv_qkv, state_delta, state_conv_ffn,
           p_prompt, p_sample, ln_in_g, ln_in_b,
           w_in, conv_a_w, w_a_out, conv_b_w, a_log, dt_bias, norm_b_g, w_b_out,
           ln_c_g, ln_c_b, w_s, b_s, w_c_out, w_o, ln1_g, ln1_b,
           w_up, conv_f_w, w_down, w_pe, w_pg, ln2_g, ln2_b):
    nbp, seq, d = x_prompt.shape
    nbs, steps, _ = x_sample.shape
    mp, ms = nbp * seq, nbs * steps

    xp = _ln_sum([(x_prompt.reshape(mp, d), 1.0)], ln_in_g, ln_in_b, 256, "ln_in")
    xs = _ln_sum([(_tm(x_sample).reshape(ms, d), 1.0)], ln_in_g, ln_in_b, 256, "ln_in_s")

    outs_p = [[] for _ in range(4)]
    outs_s = [[] for _ in range(5)]
    for i in range(DEPTH):
        w = _layer_weights(i, w_in, conv_a_w, w_a_out, conv_b_w, a_log, dt_bias, norm_b_g, w_b_out,
                           ln_c_g, ln_c_b, w_s, b_s, w_c_out, w_o, ln1_g, ln1_b,
                           w_up, conv_f_w, w_down, w_pe, w_pg, ln2_g, ln2_b)
        xp, *st_p = _layer_prompt(xp, p_prompt[i].reshape(mp, D_PLE), w, nbp, seq)
        for acc, val in zip(outs_p, st_p):
            acc.append(val)
        xs, *st_s = _layer_sample(xs, _tm(p_sample[i]).reshape(ms, D_PLE), w,
                                  state_conv_a[i], state_conv_qkv[i], state_delta[i],
                                  state_conv_ffn[i], nbs, steps)
        for acc, val in zip(outs_s, st_s):
            acc.append(val)

    y_prompt = xp.reshape(nbp, seq, d)
    y_sample = _tm(xs.reshape(steps, nbs, d))
    return (y_prompt, y_sample,
            jnp.stack(outs_p[0]), jnp.stack(outs_p[1]), jnp.stack(outs_p[2]), jnp.stack(outs_p[3]),
            jnp.stack(outs_s[0]), jnp.stack(outs_s[1]), jnp.stack(outs_s[2]), jnp.stack(outs_s[3]),
            jnp.stack(outs_s[4]))
```

```python
import functools
import math

import jax
import jax.numpy as jnp
from jax import lax
from jax.experimental import pallas as pl
from jax.experimental.pallas import tpu as pltpu

F32 = jnp.float32
BF16 = jnp.bfloat16

D_MODEL = 2048
DEPTH = 2
D_PLE = 256
D_A = D_MODEL // 2
CONV_A = 3
H_B = 8
DK = 128
DV = 128
D_B = H_B * DV
CONV_B = 4
DELTA_CHUNK = 64
D_C = D_MODEL // 2
CHUNK_C = 128
G_C = 8
D_FF = ((8 * D_MODEL // 3 + 127) // 128) * 128
CONV_F = 3
N_BRANCH = 3
DEEPNORM_ALPHA = (2 * DEPTH) ** 0.25
LN_EPS = 1e-5
RMS_EPS = 1e-6

LANES = 128
SUBLANES = 8
HIST_ROWS = SUBLANES
D_FF_PAD = 5632
VMEM_LIMIT = 48 * 1024 * 1024

OFF_AH, OFF_ABG, OFF_ACG = 0, 1024, 2048
OFF_Q, OFF_K, OFF_V = 3072, 4096, 5120
OFF_Z, OFF_CU, OFF_CV = 6144, 7168, 8192
OFF_G0, OFF_G1, OFF_G2 = 9216, 11264, 13312
N_MAIN = 15360


def _params(*sem):
    return pltpu.CompilerParams(dimension_semantics=sem, vmem_limit_bytes=VMEM_LIMIT)


def _sigmoid(x):
    return 1.0 / (1.0 + jnp.exp(-x))


def _silu(x):
    return x * _sigmoid(x)


def _gelu(x):
    c = math.sqrt(2.0 / math.pi)
    return 0.5 * x * (1.0 + jnp.tanh(c * (x + 0.044715 * (x * x * x))))


def _softplus(x):
    return jnp.maximum(x, 0.0) + jnp.log1p(jnp.exp(-jnp.abs(x)))


def _ln(x, g, b):
    mu = jnp.mean(x, axis=-1, keepdims=True)
    xc = x - mu
    var = jnp.mean(xc * xc, axis=-1, keepdims=True)
    return xc * lax.rsqrt(var + LN_EPS) * g + b


def _dot(a, b):
    return jnp.dot(a.astype(BF16), b.astype(BF16), preferred_element_type=F32)


def _dot_nt(a, b):
    return lax.dot_general(a.astype(BF16), b.astype(BF16), (((1,), (1,)), ((), ())),
                           preferred_element_type=F32)


def _dot_tn(a, b):
    return lax.dot_general(a.astype(BF16), b.astype(BF16), (((0,), (0,)), ((), ())),
                           preferred_element_type=F32)


def _dot_hi(a, b):
    return jnp.dot(a, b, precision=lax.Precision.HIGHEST, preferred_element_type=F32)


def _mm_kernel(x_ref, w_ref, o_ref):
    o_ref[...] = _dot(x_ref[...], w_ref[...]).astype(o_ref.dtype)


def _mm(x, w, bm, bn, out_dtype, name):
    m, k = x.shape
    n = w.shape[1]
    assert m % bm == 0 and n % bn == 0
    return pl.pallas_call(
        _mm_kernel,
        out_shape=jax.ShapeDtypeStruct((m, n), out_dtype),
        grid=(m // bm, n // bn),
        in_specs=[pl.BlockSpec((bm, k), lambda i, j: (i, 0)),
                  pl.BlockSpec((k, bn), lambda i, j: (0, j))],
        out_specs=pl.BlockSpec((bm, bn), lambda i, j: (i, j)),
        compiler_params=_params("parallel", "arbitrary"),
        name=name,
    )(x, w)


def _ln_sum_kernel(*refs, scales):
    n = len(scales)
    g_ref, b_ref, o_ref = refs[n], refs[n + 1], refs[n + 2]
    acc = None
    for r, s in zip(refs[:n], scales):
        v = r[...]
        if s != 1.0:
            v = s * v
        acc = v if acc is None else acc + v
    o_ref[...] = _ln(acc, g_ref[...], b_ref[...])


def _ln_sum(terms, g, b, bm, name):
    arrs = [t[0] for t in terms]
    scales = tuple(float(t[1]) for t in terms)
    m, d = arrs[0].shape
    row_spec = pl.BlockSpec((bm, d), lambda i: (i, 0))
    vec_spec = pl.BlockSpec((1, d), lambda i: (0, 0))
    return pl.pallas_call(
        functools.partial(_ln_sum_kernel, scales=scales),
        out_shape=jax.ShapeDtypeStruct((m, d), F32),
        grid=(m // bm,),
        in_specs=[row_spec] * len(arrs) + [vec_spec, vec_spec],
        out_specs=row_spec,
        compiler_params=_params("parallel"),
        name=name,
    )(*arrs, g.reshape(1, d), b.reshape(1, d))


def _conv_rows(buf_ref, x, hist, w, width):
    rows = x.shape[0]
    buf_ref[0:HIST_ROWS, :] = hist
    buf_ref[HIST_ROWS:HIST_ROWS + rows, :] = x
    out = None
    for j in range(width):
        start = HIST_ROWS - (width - 1) + j
        term = buf_ref[start:start + rows, :] * w[j:j + 1, :]
        out = term if out is None else out + term
    return out


def _conv_time_major(xs, w, width):
    n_out = len(xs) - (width - 1)
    outs = []
    for t in range(n_out):
        out = None
        for j in range(width):
            term = xs[t + j] * w[j:j + 1, :]
            out = term if out is None else out + term
        outs.append(out)
    return outs


def _mixa_p_kernel(h_ref, bg_ref, cg_ref, hist_ref, w_ref, y_ref, tail_ref, buf_ref):
    u = cg_ref[...] * h_ref[...]
    rows = u.shape[0]
    conv = _conv_rows(buf_ref, u, hist_ref[...], w_ref[...], CONV_A)
    y_ref[...] = (bg_ref[...] * conv).astype(y_ref.dtype)
    tail_ref[...] = u[rows - HIST_ROWS:, :]


def _mixa_p(proj3, hist, w8, cb=256):
    nb, seq, _ = proj3.shape
    nc = D_A // cb

    def col(off):
        return pl.BlockSpec((None, seq, cb), lambda b, j, o=off // cb: (b, 0, o + j))

    return pl.pallas_call(
        _mixa_p_kernel,
        out_shape=(jax.ShapeDtypeStruct((nb, seq, D_A), BF16),
                   jax.ShapeDtypeStruct((nb, HIST_ROWS, D_A), F32)),
        grid=(nb, nc),
        in_specs=[col(OFF_AH), col(OFF_ABG), col(OFF_ACG),
                  pl.BlockSpec((None, HIST_ROWS, cb), lambda b, j: (b, 0, j)),
                  pl.BlockSpec((HIST_ROWS, cb), lambda b, j: (0, j))],
        out_specs=(pl.BlockSpec((None, seq, cb), lambda b, j: (b, 0, j)),
                   pl.BlockSpec((None, HIST_ROWS, cb), lambda b, j: (b, 0, j))),
        scratch_shapes=[pltpu.VMEM((seq + HIST_ROWS, cb), F32)],
        compiler_params=_params("parallel", "parallel"),
        name="mixa_prompt",
    )(proj3, proj3, proj3, hist, w8)


def _mixa_s_kernel(h_ref, bg_ref, cg_ref, hist_ref, w_ref, y_ref, u_ref):
    steps = h_ref.shape[0]
    us = []
    for t in range(steps):
        u = cg_ref[t] * h_ref[t]
        u_ref[t] = u
        us.append(u)
    xs = [hist_ref[t] for t in range(CONV_A - 1)] + us
    convs = _conv_time_major(xs, w_ref[...], CONV_A)
    for t in range(steps):
        y_ref[t] = (bg_ref[t] * convs[t]).astype(y_ref.dtype)


def _mixa_s(proj4, hist_tm, w8, cb=512):
    steps, nb, _ = proj4.shape
    nc = D_A // cb

    def col(off):
        return pl.BlockSpec((steps, nb, cb), lambda j, o=off // cb: (0, 0, o + j))

    return pl.pallas_call(
        _mixa_s_kernel,
        out_shape=(jax.ShapeDtypeStruct((steps, nb, D_A), BF16),
                   jax.ShapeDtypeStruct((steps, nb, D_A), F32)),
        grid=(nc,),
        in_specs=[col(OFF_AH), col(OFF_ABG), col(OFF_ACG),
                  pl.BlockSpec((CONV_A - 1, nb, cb), lambda j: (0, 0, j)),
                  pl.BlockSpec((HIST_ROWS, cb), lambda j: (0, j))],
        out_specs=(pl.BlockSpec((steps, nb, cb), lambda j: (0, 0, j)),
                   pl.BlockSpec((steps, nb, cb), lambda j: (0, 0, j))),
        compiler_params=_params("parallel"),
        name="mixa_sample",
    )(proj4, proj4, proj4, hist_tm, w8)


def _l2n(t):
    return t * lax.rsqrt(jnp.sum(t * t, axis=-1, keepdims=True) + RMS_EPS)


def _qkv_p_kernel(q_ref, k_ref, v_ref, hq_ref, hk_ref, hv_ref, wq_ref, wk_ref, wv_ref,
                  qo_ref, ko_ref, vo_ref, buf_ref):
    q = _silu(_conv_rows(buf_ref, q_ref[...], hq_ref[...], wq_ref[...], CONV_B))
    qo_ref[...] = _l2n(q) * (DK ** -0.5)
    k = _silu(_conv_rows(buf_ref, k_ref[...], hk_ref[...], wk_ref[...], CONV_B))
    ko_ref[...] = _l2n(k)
    vo_ref[...] = _silu(_conv_rows(buf_ref, v_ref[...], hv_ref[...], wv_ref[...], CONV_B))


def _qkv_p(proj3, hist, w8):
    nb, seq, _ = proj3.shape
    cb = DK
    nh = D_B // cb

    def col(off):
        return pl.BlockSpec((None, seq, cb), lambda b, j, o=off // cb: (b, 0, o + j))

    def hcol(part):
        return pl.BlockSpec((None, HIST_ROWS, cb), lambda b, j, o=part * nh: (b, 0, o + j))

    def wcol(part):
        return pl.BlockSpec((HIST_ROWS, cb), lambda b, j, o=part * nh: (0, o + j))

    out = jax.ShapeDtypeStruct((nb, seq, D_B), F32)
    ospec = pl.BlockSpec((None, seq, cb), lambda b, j: (b, 0, j))
    return pl.pallas_call(
        _qkv_p_kernel,
        out_shape=(out, out, out),
        grid=(nb, nh),
        in_specs=[col(OFF_Q), col(OFF_K), col(OFF_V), hcol(0), hcol(1), hcol(2),
                  wcol(0), wcol(1), wcol(2)],
        out_specs=(ospec, ospec, ospec),
        scratch_shapes=[pltpu.VMEM((seq + HIST_ROWS, cb), F32)],
        compiler_params=_params("parallel", "parallel"),
        name="qkv_prompt",
    )(proj3, proj3, proj3, hist, hist, hist, w8, w8, w8)


def _qkv_s_kernel(q_ref, k_ref, v_ref, hq_ref, hk_ref, hv_ref, wq_ref, wk_ref, wv_ref,
                  qo_ref, ko_ref, vo_ref):
    steps = q_ref.shape[0]

    def conv(x_ref, h_ref, w_ref):
        xs = [h_ref[t] for t in range(CONV_B - 1)] + [x_ref[t] for t in range(steps)]
        return _conv_time_major(xs, w_ref[...], CONV_B)

    for t, c in enumerate(conv(q_ref, hq_ref, wq_ref)):
        qo_ref[t] = _l2n(_silu(c)) * (DK ** -0.5)
    for t, c in enumerate(conv(k_ref, hk_ref, wk_ref)):
        ko_ref[t] = _l2n(_silu(c))
    for t, c in enumerate(conv(v_ref, hv_ref, wv_ref)):
        vo_ref[t] = _silu(c)


def _qkv_s(proj4, hist_tm, w8):
    steps, nb, _ = proj4.shape
    cb = DK
    nh = D_B // cb

    def col(off):
        return pl.BlockSpec((steps, nb, cb), lambda j, o=off // cb: (0, 0, o + j))

    def hcol(part):
        return pl.BlockSpec((CONV_B - 1, nb, cb), lambda j, o=part * nh: (0, 0, o + j))

    def wcol(part):
        return pl.BlockSpec((HIST_ROWS, cb), lambda j, o=part * nh: (0, o + j))

    out = jax.ShapeDtypeStruct((steps, nb, D_B), F32)
    ospec = pl.BlockSpec((steps, nb, cb), lambda j: (0, 0, j))
    return pl.pallas_call(
        _qkv_s_kernel,
        out_shape=(out, out, out),
        grid=(nh,),
        in_specs=[col(OFF_Q), col(OFF_K), col(OFF_V), hcol(0), hcol(1), hcol(2),
                  wcol(0), wcol(1), wcol(2)],
        out_specs=(ospec, ospec, ospec),
        compiler_params=_params("parallel"),
        name="qkv_sample",
    )(proj4, proj4, proj4, hist_tm, hist_tm, hist_tm, w8, w8, w8)


def _neumann(xs, eye, order):
    ps = [eye - x for x in xs]
    xp = xs
    n = 2
    while n < order:
        xp = [_dot(x, x) for x in xp]
        ps = [p + _dot(p, x) for p, x in zip(ps, xp)]
        n *= 2
    return ps


def _unit_lower_inverse(lmats, chunk, blk, order):
    row = lax.broadcasted_iota(jnp.int32, (chunk, chunk), 0)
    col = lax.broadcasted_iota(jnp.int32, (chunk, chunk), 1)
    eye = (row == col).astype(F32)
    if order <= blk:
        return _neumann(lmats, eye, order)
    same = jnp.bitwise_xor(row, col) < blk
    d_inv = _neumann([jnp.where(same, l, 0.0) for l in lmats], eye, blk)
    ms = [_dot(d, jnp.where(same, 0.0, l)) for d, l in zip(d_inv, lmats)]
    m_inv = _neumann(ms, eye, order // blk)
    return [_dot(mi, d) for mi, d in zip(m_inv, d_inv)]


def _delta_kernel(q_ref, k_ref, v_ref, bg_ref, z_ref, alog_ref, dtb_ref, ng_ref, s0_ref,
                  y_ref, sfin_ref, s_scr, *, nseq, chunk, valid, n_chunks, blk, order):
    c = pl.program_id(1)

    @pl.when(c == 0)
    def _():
        s_scr[...] = s0_ref[...]

    row = lax.broadcasted_iota(jnp.int32, (chunk, chunk), 0)
    col = lax.broadcasted_iota(jnp.int32, (chunk, chunk), 1)
    causal = row >= col
    strict = row > col
    tri = causal.astype(F32)
    neg_a = -jnp.exp(alog_ref[...])
    dtb = dtb_ref[...]
    ng = ng_ref[...]

    betas, gcums, gcum_ts = [], [], []
    for s in range(nseq):
        beta = _sigmoid(bg_ref[s, :, 0:LANES])
        g = neg_a * _softplus(bg_ref[s, :, LANES:2 * LANES] + dtb)
        if valid < chunk:
            live = lax.broadcasted_iota(jnp.int32, (chunk, LANES), 0) < valid
            beta = jnp.where(live, beta, 0.0)
            g = jnp.where(live, g, 0.0)
        gcum = _dot_hi(tri, g)
        gpad = jnp.concatenate([gcum, jnp.zeros((LANES - chunk, LANES), F32)], axis=0)
        betas.append(beta)
        gcums.append(gcum)
        gcum_ts.append(gpad.T)

    probs = [(s, h) for s in range(nseq) for h in range(H_B)]
    cols = [slice(h * DK, (h + 1) * DK) for _, h in probs]
    qh = [q_ref[s, :, sl] for (s, _), sl in zip(probs, cols)]
    kh = [k_ref[s, :, sl] for (s, _), sl in zip(probs, cols)]
    vh = [v_ref[s, :, sl] for (s, _), sl in zip(probs, cols)]
    bcol = [betas[s][:, h:h + 1] for s, h in probs]
    gcol = [gcums[s][:, h:h + 1] for s, h in probs]
    glast = [gcums[s][chunk - 1:chunk, h:h + 1] for s, h in probs]
    decay = [jnp.exp(jnp.where(causal, gcums[s][:, h:h + 1] - gcum_ts[s][h:h + 1, 0:chunk], -jnp.inf))
             for s, h in probs]
    eg = [jnp.exp(g) for g in gcol]
    kb = [k * b for k, b in zip(kh, bcol)]

    kq = [_dot_nt(jnp.concatenate([a, q], axis=0), k) for a, q, k in zip(kb, qh, kh)]
    lmat = [jnp.where(strict, m[:chunk] * d, 0.0) for m, d in zip(kq, decay)]
    qk = [m[chunk:] * d for m, d in zip(kq, decay)]
    tmat = _unit_lower_inverse(lmat, chunk, blk, order)
    uw = [_dot(t, jnp.concatenate([v * b, a * e], axis=1))
          for t, v, b, a, e in zip(tmat, vh, bcol, kb, eg)]
    st = [s_scr[s, h] for s, h in probs]
    wq = [_dot(jnp.concatenate([m[:, DV:], q * e], axis=0), s)
          for m, q, e, s in zip(uw, qh, eg, st)]
    v_new = [m[:, :DV] - x[:chunk] for m, x in zip(uw, wq)]
    o = [x[chunk:] + _dot(a, vn) for x, a, vn in zip(wq, qk, v_new)]
    s_new = [s * jnp.exp(gl) + _dot_tn(k * jnp.exp(gl - g), vn)
             for s, gl, k, g, vn in zip(st, glast, kh, gcol, v_new)]
    for (s, h), sl, sn, oo in zip(probs, cols, s_new, o):
        s_scr[s, h] = sn
        on = oo * lax.rsqrt(jnp.mean(oo * oo, axis=-1, keepdims=True) + RMS_EPS) * ng
        y_ref[s, :, sl] = (on * _silu(z_ref[s, :, sl])).astype(y_ref.dtype)

    @pl.when(c == n_chunks - 1)
    def _():
        sfin_ref[...] = s_scr[...]


def _delta(q, k, v, bg, z3, zcol, alog, dtb, ng, s0_all, layer, nseq, chunk, valid, blk, name):
    nb, seq, _ = q.shape
    n_chunks = seq // chunk
    order = chunk
    while order // 2 >= valid:
        order //= 2
    tok = pl.BlockSpec((nseq, chunk, D_B), lambda b, c: (b, c, 0))
    vec = pl.BlockSpec((1, LANES), lambda b, c: (0, 0))
    st = pl.BlockSpec((nseq, H_B, DK, DV), lambda b, c: (b, 0, 0, 0))
    st_in = pl.BlockSpec((None, nseq, H_B, DK, DV), lambda b, c, i=layer: (i, b, 0, 0, 0))
    return pl.pallas_call(
        functools.partial(_delta_kernel, nseq=nseq, chunk=chunk, valid=valid, n_chunks=n_chunks,
                          blk=blk, order=order),
        out_shape=(jax.ShapeDtypeStruct((nb, seq, D_B), BF16),
                   jax.ShapeDtypeStruct((nb, H_B, DK, DV), F32)),
        grid=(nb // nseq, n_chunks),
        in_specs=[tok, tok, tok,
                  pl.BlockSpec((nseq, chunk, 2 * LANES), lambda b, c: (b, c, 0)),
                  pl.BlockSpec((nseq, chunk, D_B), lambda b, c, o=zcol: (b, c, o)),
                  vec, vec, vec, st_in],
        out_specs=(tok, st),
        scratch_shapes=[pltpu.VMEM((nseq, H_B, DK, DV), F32)],
        compiler_params=_params("parallel", "arbitrary"),
        name=name,
    )(q, k, v, bg, z3, alog, dtb, ng, s0_all)


def _mixc_p_kernel(cu_ref, cv_ref, g_ref, b_ref, ws_ref, bias_ref, y_ref):
    vc = _ln(_gelu(cv_ref[...]), g_ref[...], b_ref[...])
    u = _gelu(cu_ref[...])
    row = lax.broadcasted_iota(jnp.int32, (CHUNK_C, CHUNK_C), 0)
    col = lax.broadcasted_iota(jnp.int32, (CHUNK_C, CHUNK_C), 1)
    tri = row >= col
    cg = D_C // G_C
    for g in range(G_C):
        sl = slice(g * cg, (g + 1) * cg)
        wm = jnp.where(tri, ws_ref[g], 0.0)
        mixed = _dot(wm, vc[:, sl]) + bias_ref[:, sl]
        y_ref[:, sl] = (u[:, sl] * mixed).astype(y_ref.dtype)


def _mixc_p(proj3, g, b, ws, bias):
    nb, seq, _ = proj3.shape
    vec = pl.BlockSpec((1, D_C), lambda bb, n: (0, 0))
    return pl.pallas_call(
        _mixc_p_kernel,
        out_shape=jax.ShapeDtypeStruct((nb, seq, D_C), BF16),
        grid=(nb, seq // CHUNK_C),
        in_specs=[pl.BlockSpec((None, CHUNK_C, D_C), lambda bb, n: (bb, n, OFF_CU // D_C)),
                  pl.BlockSpec((None, CHUNK_C, D_C), lambda bb, n: (bb, n, OFF_CV // D_C)),
                  vec, vec,
                  pl.BlockSpec((G_C, CHUNK_C, CHUNK_C), lambda bb, n: (0, 0, 0)),
                  pl.BlockSpec((CHUNK_C, D_C), lambda bb, n: (0, 0))],
        out_specs=pl.BlockSpec((None, CHUNK_C, D_C), lambda bb, n: (bb, n, 0)),
        compiler_params=_params("parallel", "parallel"),
        name="mixc_prompt",
    )(proj3, proj3, g, b, ws, bias)


def _mixc_s_kernel(cu_ref, cv_ref, g_ref, b_ref, wc_ref, bias_ref, y_ref, vc_ref):
    steps = cu_ref.shape[0]
    vcs = []
    for t in range(steps):
        vc = _ln(_gelu(cv_ref[t]), g_ref[...], b_ref[...])
        vc_ref[t] = vc
        vcs.append(vc)
    for t in range(steps):
        acc = None
        for s in range(t + 1):
            term = wc_ref[t * steps + s:t * steps + s + 1, :] * vcs[s]
            acc = term if acc is None else acc + term
        mixed = acc + bias_ref[t:t + 1, :]
        y_ref[t] = (_gelu(cu_ref[t]) * mixed).astype(y_ref.dtype)


def _mixc_s(proj4, g, b, wc, bias):
    steps, nb, _ = proj4.shape
    full = lambda shape: pl.BlockSpec(shape, lambda i: (0,) * len(shape))
    return pl.pallas_call(
        _mixc_s_kernel,
        out_shape=(jax.ShapeDtypeStruct((steps, nb, D_C), BF16),
                   jax.ShapeDtypeStruct((steps, nb, D_C), F32)),
        grid=(1,),
        in_specs=[pl.BlockSpec((steps, nb, D_C), lambda i: (0, 0, OFF_CU // D_C)),
                  pl.BlockSpec((steps, nb, D_C), lambda i: (0, 0, OFF_CV // D_C)),
                  full((1, D_C)), full((1, D_C)), full(wc.shape), full(bias.shape)],
        out_specs=(full((steps, nb, D_C)), full((steps, nb, D_C))),
        compiler_params=_params("arbitrary"),
        name="mixc_sample",
    )(proj4, proj4, g, b, wc, bias)


def _merge_kernel(ya_ref, yb_ref, yc_ref, wa_ref, wb_ref, wc_ref, g0_ref, g1_ref, g2_ref, o_ref):
    m = _sigmoid(g0_ref[...]) * _dot(ya_ref[...], wa_ref[...])
    m = m + _sigmoid(g1_ref[...]) * _dot(yb_ref[...], wb_ref[...])
    m = m + _sigmoid(g2_ref[...]) * _dot(yc_ref[...], wc_ref[...])
    o_ref[...] = m.astype(o_ref.dtype)


def _merge(ya, yb, yc, wa, wb, wc, proj, bm=512, bn=512):
    m = ya.shape[0]
    xs = pl.BlockSpec((bm, D_A), lambda i, j: (i, 0))
    ws = pl.BlockSpec((D_A, bn), lambda i, j: (0, j))

    def gate(off):
        return pl.BlockSpec((bm, bn), lambda i, j, o=off // bn: (i, o + j))

    return pl.pallas_call(
        _merge_kernel,
        out_shape=jax.ShapeDtypeStruct((m, D_MODEL), BF16),
        grid=(m // bm, D_MODEL // bn),
        in_specs=[xs, xs, xs, ws, ws, ws, gate(OFF_G0), gate(OFF_G1), gate(OFF_G2)],
        out_specs=pl.BlockSpec((bm, bn), lambda i, j: (i, j)),
        compiler_params=_params("parallel", "arbitrary"),
        name="merge",
    )(ya, yb, yc, wa, wb, wc, proj, proj, proj)


def _ffn_p_kernel(hg_ref, hu_ref, hist_ref, w_ref, o_ref, buf_ref):
    conv = _conv_rows(buf_ref, hg_ref[...], hist_ref[...], w_ref[...], CONV_F)
    o_ref[...] = (_silu(conv) * hu_ref[...]).astype(o_ref.dtype)


def _ffn_p(h3, hist, w8, cb=256):
    nb, seq, _ = h3.shape
    nc = D_FF_PAD // cb
    return pl.pallas_call(
        _ffn_p_kernel,
        out_shape=jax.ShapeDtypeStruct((nb, seq, D_FF_PAD), BF16),
        grid=(nb, nc),
        in_specs=[pl.BlockSpec((None, seq, cb), lambda b, j: (b, 0, j)),
                  pl.BlockSpec((None, seq, cb), lambda b, j: (b, 0, nc + j)),
                  pl.BlockSpec((None, HIST_ROWS, cb), lambda b, j: (b, 0, j)),
                  pl.BlockSpec((HIST_ROWS, cb), lambda b, j: (0, j))],
        out_specs=pl.BlockSpec((None, seq, cb), lambda b, j: (b, 0, j)),
        scratch_shapes=[pltpu.VMEM((seq + HIST_ROWS, cb), F32)],
        compiler_params=_params("parallel", "parallel"),
        name="ffnact_prompt",
    )(h3, h3, hist, w8)


def _ffn_s_kernel(hg_ref, hu_ref, hist_ref, w_ref, o_ref):
    steps = hg_ref.shape[0]
    xs = [hist_ref[t] for t in range(CONV_F - 1)] + [hg_ref[t] for t in range(steps)]
    for t, c in enumerate(_conv_time_major(xs, w_ref[...], CONV_F)):
        o_ref[t] = (_silu(c) * hu_ref[t]).astype(o_ref.dtype)


def _ffn_s(h4, hist_tm, w8, cb=512):
    steps, nb, _ = h4.shape
    nc = D_FF_PAD // cb
    return pl.pallas_call(
        _ffn_s_kernel,
        out_shape=jax.ShapeDtypeStruct((steps, nb, D_FF_PAD), BF16),
        grid=(nc,),
        in_specs=[pl.BlockSpec((steps, nb, cb), lambda j: (0, 0, j)),
                  pl.BlockSpec((steps, nb, cb), lambda j: (0, 0, nc + j)),
                  pl.BlockSpec((CONV_F - 1, nb, cb), lambda j: (0, 0, j)),
                  pl.BlockSpec((HIST_ROWS, cb), lambda j: (0, j))],
        out_specs=pl.BlockSpec((steps, nb, cb), lambda j: (0, 0, j)),
        compiler_params=_params("parallel"),
        name="ffnact_sample",
    )(h4, h4, hist_tm, w8)


def _ple_kernel(x_ref, wpg_ref, p_ref, wpe_ref, o_ref):
    o_ref[...] = _sigmoid(_dot(x_ref[...], wpg_ref[...])) * _dot(p_ref[...], wpe_ref[...])


def _ple(x, wpg, p, wpe, bm=512, bn=512):
    m = x.shape[0]
    return pl.pallas_call(
        _ple_kernel,
        out_shape=jax.ShapeDtypeStruct((m, D_MODEL), F32),
        grid=(m // bm, D_MODEL // bn),
        in_specs=[pl.BlockSpec((bm, D_MODEL), lambda i, j: (i, 0)),
                  pl.BlockSpec((D_MODEL, bn), lambda i, j: (0, j)),
                  pl.BlockSpec((bm, D_PLE), lambda i, j: (i, 0)),
                  pl.BlockSpec((D_PLE, bn), lambda i, j: (0, j))],
        out_specs=pl.BlockSpec((bm, bn), lambda i, j: (i, j)),
        compiler_params=_params("parallel", "arbitrary"),
        name="ple",
    )(x, wpg, p, wpe)


def _pad_rows(w, rows):
    return jnp.pad(w, ((0, rows - w.shape[0]), (0, 0)))


def _pad_cols(w, cols):
    return jnp.pad(w, ((0, 0),) * (w.ndim - 1) + ((0, cols - w.shape[-1]),))


def _layer_weights(i, w_in, conv_a_w, w_a_out, conv_b_w, a_log, dt_bias, norm_b_g, w_b_out,
                   ln_c_g, ln_c_b, w_s, b_s, w_c_out, w_o, ln1_g, ln1_b,
                   w_up, conv_f_w, w_down, w_pe, w_pg, ln2_g, ln2_b):
    off_beta = 3 * D_A + 3 * D_B + D_B
    off_dec = off_beta + H_B
    off_cu = off_dec + H_B
    wi = w_in[i]
    w = {}
    w["main"] = jnp.concatenate([wi[:, :off_beta], wi[:, off_cu:]], axis=1).astype(BF16)
    w["small"] = jnp.concatenate([_pad_cols(wi[:, off_beta:off_dec], LANES),
                                  _pad_cols(wi[:, off_dec:off_cu], LANES)], axis=1).astype(BF16)
    w["conv_a"] = _pad_rows(conv_a_w[i], HIST_ROWS)
    w["conv_b"] = _pad_rows(conv_b_w[i], HIST_ROWS)
    w["conv_f"] = _pad_cols(_pad_rows(conv_f_w[i], HIST_ROWS), D_FF_PAD)
    w["a_out"] = w_a_out[i].astype(BF16)
    w["b_out"] = w_b_out[i].astype(BF16)
    w["c_out"] = w_c_out[i].astype(BF16)
    w["alog"] = _pad_cols(a_log[i].reshape(1, H_B), LANES)
    w["dtb"] = _pad_cols(dt_bias[i].reshape(1, H_B), LANES)
    w["ng"] = norm_b_g[i].reshape(1, DV)
    w["lncg"] = ln_c_g[i].reshape(1, D_C)
    w["lncb"] = ln_c_b[i].reshape(1, D_C)
    w["ws"] = w_s[i]
    cg = D_C // G_C
    w["bias_c"] = jnp.repeat(b_s[i].T, cg, axis=1)
    w["o"] = w_o[i].astype(BF16)
    w["ln1g"], w["ln1b"] = ln1_g[i], ln1_b[i]
    wu = w_up[i]
    w["up"] = jnp.concatenate([_pad_cols(wu[:, :D_FF], D_FF_PAD),
                               _pad_cols(wu[:, D_FF:], D_FF_PAD)], axis=1).astype(BF16)
    w["down"] = _pad_rows(w_down[i], D_FF_PAD).astype(BF16)
    w["pe"] = w_pe[i].astype(BF16)
    w["pg"] = w_pg[i].astype(BF16)
    w["ln2g"], w["ln2b"] = ln2_g[i], ln2_b[i]
    return w


def _sample_mix_coeffs(w, steps):
    cg = D_C // G_C
    wc = jnp.transpose(w["ws"][:, :steps, :steps], (1, 2, 0))
    wc = jnp.repeat(wc, cg, axis=2).reshape(steps * steps, D_C)
    return wc, w["bias_c"][:HIST_ROWS]


def _layer_prompt(x, p2d, w, nb, seq):
    m = x.shape[0]
    proj = _mm(x, w["main"], 1024, 512, F32, "in_proj")
    small = _mm(x, w["small"], 1024, 2 * LANES, F32, "in_proj_small")
    proj3 = proj.reshape(nb, seq, N_MAIN)

    ya, tail_a = _mixa_p(proj3, jnp.zeros((nb, HIST_ROWS, D_A), F32), w["conv_a"])
    q, k, v = _qkv_p(proj3, jnp.zeros((nb, HIST_ROWS, 3 * D_B), F32), w["conv_b"])
    yb, s_fin = _delta(q, k, v, small.reshape(nb, seq, 2 * LANES), proj3, OFF_Z // D_B,
                       w["alog"], w["dtb"], w["ng"], jnp.zeros((1, nb, H_B, DK, DV), F32), 0,
                       1, DELTA_CHUNK, DELTA_CHUNK, SUBLANES, "delta_prompt")
    yc = _mixc_p(proj3, w["lncg"], w["lncb"], w["ws"], w["bias_c"])

    merged = _merge(ya.reshape(m, D_A), yb.reshape(m, D_B), yc.reshape(m, D_C),
                    w["a_out"], w["b_out"], w["c_out"], proj)
    mo = _mm(merged, w["o"], 1024, 512, F32, "o_proj")
    x1 = _ln_sum([(x, DEEPNORM_ALPHA), (mo, 1.0)], w["ln1g"], w["ln1b"], 256, "ln1")

    h = _mm(x1, w["up"], 1024, 512, F32, "up_proj")
    h3 = h.reshape(nb, seq, 2 * D_FF_PAD)
    act = _ffn_p(h3, jnp.zeros((nb, HIST_ROWS, D_FF_PAD), F32), w["conv_f"])
    ffn = _mm(act.reshape(m, D_FF_PAD), w["down"], 512, 512, F32, "down_proj")
    ple = _ple(x1, w["pg"], p2d, w["pe"])
    x2 = _ln_sum([(x1, DEEPNORM_ALPHA), (ffn, 1.0), (ple, 1.0)], w["ln2g"], w["ln2b"], 256, "ln2")

    new_a = tail_a[:, HIST_ROWS - (CONV_A - 1):, :]
    new_q = proj3[:, seq - (CONV_B - 1):, OFF_Q:OFF_Q + 3 * D_B]
    new_f = h3[:, seq - (CONV_F - 1):, :D_FF]
    return x2, new_a, new_q, s_fin, new_f


def _tm(a):
    return jnp.swapaxes(a, 0, 1)


def _layer_sample(x, p2d, w, st_a, st_q, st_d_all, layer, st_f, nb, steps):
    m = x.shape[0]
    chunk = 2 * SUBLANES
    proj = _mm(x, w["main"], m, 512, F32, "in_proj_s")
    small = _mm(x, w["small"], m, 2 * LANES, F32, "in_proj_small_s")
    proj4 = proj.reshape(steps, nb, N_MAIN)

    ya, u = _mixa_s(proj4, _tm(st_a), w["conv_a"])
    q, k, v = _qkv_s(proj4, _tm(st_q), w["conv_b"])

    def to_chunks(a_tm):
        return jnp.pad(_tm(a_tm), ((0, 0), (0, chunk - steps), (0, 0)))

    yb_b, s_fin = _delta(to_chunks(q), to_chunks(k), to_chunks(v),
                         to_chunks(small.reshape(steps, nb, 2 * LANES)),
                         to_chunks(proj4[:, :, OFF_Z:OFF_Z + D_B]), 0,
                         w["alog"], w["dtb"], w["ng"], st_d_all, layer,
                         4, chunk, steps, SUBLANES, "delta_sample")
    yb = _tm(yb_b[:, :steps]).reshape(m, D_B)

    wc, bias = _sample_mix_coeffs(w, steps)
    yc, vc = _mixc_s(proj4, w["lncg"], w["lncb"], wc, bias)

    merged = _merge(ya.reshape(m, D_A), yb, yc.reshape(m, D_C),
                    w["a_out"], w["b_out"], w["c_out"], proj, bm=m)
    mo = _mm(merged, w["o"], m, 512, F32, "o_proj_s")
    x1 = _ln_sum([(x, DEEPNORM_ALPHA), (mo, 1.0)], w["ln1g"], w["ln1b"], 256, "ln1_s")

    h = _mm(x1, w["up"], m, 512, F32, "up_proj_s")
    h4 = h.reshape(steps, nb, 2 * D_FF_PAD)
    act = _ffn_s(h4, _pad_cols(_tm(st_f), D_FF_PAD), w["conv_f"])
    ffn = _mm(act.reshape(m, D_FF_PAD), w["down"], m, 512, F32, "down_proj_s")
    ple = _ple(x1, w["pg"], p2d, w["pe"], bm=m)
    x2 = _ln_sum([(x1, DEEPNORM_ALPHA), (ffn, 1.0), (ple, 1.0)], w["ln2g"], w["ln2b"], 256, "ln2_s")

    new_a = _tm(u[steps - (CONV_A - 1):])
    new_q = _tm(proj4[steps - (CONV_B - 1):, :, OFF_Q:OFF_Q + 3 * D_B])
    new_f = _tm(h4[steps - (CONV_F - 1):, :, :D_FF])
    return x2, new_a, new_q, s_fin, new_f, _tm(vc)


def kernel(x_prompt, x_sample, state_conv_a, state_conv_qkv, state_delta, state_conv_ffn,
           p_prompt, p_sample, ln_in_g, ln_in_b,
           w_in, conv_a_w, w_a_out, conv_b_w, a_log, dt_bias, norm_b_g, w_b_out,
           ln_c_g, ln_c_b, w_s, b_s, w_c_out, w_o, ln1_g, ln1_b,
           w_up, conv_f_w, w_down, w_pe, w_pg, ln2_g, ln2_b):
    nbp, seq, d = x_prompt.shape
    nbs, steps, _ = x_sample.shape
    mp, ms = nbp * seq, nbs * steps

    xp = _ln_sum([(x_prompt.reshape(mp, d), 1.0)], ln_in_g, ln_in_b, 256, "ln_in")
    xs = _ln_sum([(_tm(x_sample).reshape(ms, d), 1.0)], ln_in_g, ln_in_b, 256, "ln_in_s")

    outs_p = [[] for _ in range(4)]
    outs_s = [[] for _ in range(5)]
    for i in range(DEPTH):
        w = _layer_weights(i, w_in, conv_a_w, w_a_out, conv_b_w, a_log, dt_bias, norm_b_g, w_b_out,
                           ln_c_g, ln_c_b, w_s, b_s, w_c_out, w_o, ln1_g, ln1_b,
                           w_up, conv_f_w, w_down, w_pe, w_pg, ln2_g, ln2_b)
        xp, *st_p = _layer_prompt(xp, p_prompt[i].reshape(mp, D_PLE), w, nbp, seq)
        for acc, val in zip(outs_p, st_p):
            acc.append(val)
        xs, *st_s = _layer_sample(xs, _tm(p_sample[i]).reshape(ms, D_PLE), w,
                                  state_conv_a[i], state_conv_qkv[i], state_delta, i,
                                  state_conv_ffn[i], nbs, steps)
        for acc, val in zip(outs_s, st_s):
            acc.append(val)

    y_prompt = xp.reshape(nbp, seq, d)
    y_sample = _tm(xs.reshape(steps, nbs, d))
    return (y_prompt, y_sample,
            jnp.stack(outs_p[0]), jnp.stack(outs_p[1]), jnp.stack(outs_p[2]), jnp.stack(outs_p[3]),
            jnp.stack(outs_s[0]), jnp.stack(outs_s[1]), jnp.stack(outs_s[2]), jnp.stack(outs_s[3]),
            jnp.stack(outs_s[4]))
```

```python
import functools
import math
from typing import NamedTuple

import jax
import jax.numpy as jnp
from jax import lax
from jax.experimental import pallas as pl
from jax.experimental.pallas import tpu as pltpu

F32 = jnp.float32
BF16 = jnp.bfloat16

D_MODEL = 2048
DEPTH = 2
D_PLE = 256
D_A = D_MODEL // 2
CONV_A = 3
H_B = 8
DK = 128
DV = 128
D_B = H_B * DV
CONV_B = 4
DELTA_CHUNK = 64
D_C = D_MODEL // 2
CHUNK_C = 128
G_C = 8
D_FF = ((8 * D_MODEL // 3 + 127) // 128) * 128
CONV_F = 3
N_BRANCH = 3
DEEPNORM_ALPHA = (2 * DEPTH) ** 0.25
LN_EPS = 1e-5
RMS_EPS = 1e-6

LANES = 128
SUBLANES = 8
BF16_ROWS = 16
D_FF_PAD = 5632
D_QKVZ = 4 * D_B
VMEM_LIMIT = 48 * 1024 * 1024

OFF_QKV = 3 * D_A
OFF_BETA = OFF_QKV + 4 * D_B
OFF_DEC = OFF_BETA + H_B
OFF_CU = OFF_DEC + H_B


class _Rows(NamedTuple):
    bm: int
    stride: int
    bps: int
    recompute: bool

    def halo(self, width):
        return BF16_ROWS if self.recompute else (width - 1) * self.stride

    def tail(self, width):
        return SUBLANES if self.recompute else (width - 1) * self.stride


def _params(*sem):
    return pltpu.CompilerParams(dimension_semantics=sem, vmem_limit_bytes=VMEM_LIMIT)


def _sigmoid(x):
    return 1.0 / (1.0 + jnp.exp(-x))


def _silu(x):
    return x * _sigmoid(x)


def _gelu(x):
    c = math.sqrt(2.0 / math.pi)
    return 0.5 * x * (1.0 + jnp.tanh(c * (x + 0.044715 * (x * x * x))))


def _softplus(x):
    return jnp.maximum(x, 0.0) + jnp.log1p(jnp.exp(-jnp.abs(x)))


def _ln(x, g, b):
    mu = jnp.mean(x, axis=-1, keepdims=True)
    xc = x - mu
    var = jnp.mean(xc * xc, axis=-1, keepdims=True)
    return xc * lax.rsqrt(var + LN_EPS) * g + b


def _l2n(t):
    return t * lax.rsqrt(jnp.sum(t * t, axis=-1, keepdims=True) + RMS_EPS)


def _dot(a, b):
    return jnp.dot(a.astype(BF16), b.astype(BF16), preferred_element_type=F32)


def _dot_nt(a, b):
    return lax.dot_general(a.astype(BF16), b.astype(BF16), (((1,), (1,)), ((), ())),
                           preferred_element_type=F32)


def _dot_tn(a, b):
    return lax.dot_general(a.astype(BF16), b.astype(BF16), (((0,), (0,)), ((), ())),
                           preferred_element_type=F32)


def _dot_hi(a, b):
    return jnp.dot(a, b, precision=lax.Precision.HIGHEST, preferred_element_type=F32)


def _mm_kernel(x_ref, w_ref, o_ref):
    o_ref[...] = _dot(x_ref[...], w_ref[...]).astype(o_ref.dtype)


def _mm(x, w, bm, bn, out_dtype, name):
    m, k = x.shape
    n = w.shape[1]
    assert m % bm == 0 and n % bn == 0
    return pl.pallas_call(
        _mm_kernel,
        out_shape=jax.ShapeDtypeStruct((m, n), out_dtype),
        grid=(m // bm, n // bn),
        in_specs=[pl.BlockSpec((bm, k), lambda i, j: (i, 0)),
                  pl.BlockSpec((k, bn), lambda i, j: (0, j))],
        out_specs=pl.BlockSpec((bm, bn), lambda i, j: (i, j)),
        compiler_params=_params("parallel", "arbitrary"),
        name=name,
    )(x, w)


def _ln_in_kernel(x_ref, g_ref, b_ref, o_ref, obf_ref):
    r = _ln(x_ref[...], g_ref[...], b_ref[...])
    o_ref[...] = r
    obf_ref[...] = r.astype(BF16)


def _ln_in(x, g, b, bm, name):
    m, d = x.shape
    row = pl.BlockSpec((bm, d), lambda i: (i, 0))
    vec = pl.BlockSpec((1, d), lambda i: (0, 0))
    return pl.pallas_call(
        _ln_in_kernel,
        out_shape=(jax.ShapeDtypeStruct((m, d), F32), jax.ShapeDtypeStruct((m, d), BF16)),
        grid=(m // bm,),
        in_specs=[row, vec, vec],
        out_specs=(row, row),
        compiler_params=_params("parallel"),
        name=name,
    )(x, g.reshape(1, d), b.reshape(1, d))


def _stage_lhs(xh_ref, xm_ref, xcat_ref, halo):
    @pl.when(pl.program_id(1) == 0)
    def _():
        xcat_ref[0:halo, :] = xh_ref[...]
        xcat_ref[halo:, :] = xm_ref[...]
    return xcat_ref[...]


def _split_halo(res, hist_ref, rows, halo):
    is_start = (pl.program_id(0) % rows.bps) == 0
    return jnp.where(is_start, hist_ref[...], res[:halo]), res[halo:]


def _conv_from_buf(buf_ref, top, main, cw, width, halo, stride):
    bm = main.shape[0]
    buf_ref[0:halo, :] = top
    buf_ref[halo:halo + bm, :] = main
    out = None
    for j in range(width):
        start = halo - (width - 1 - j) * stride
        term = buf_ref[start:start + bm, :] * cw[j:j + 1, :]
        out = term if out is None else out + term
    return out


def _conv_call(kernel, rows, width, x_bf, weights, w_cols, hist, cw, n_cols, bn, out_dtype, name):
    m, k = x_bf.shape
    bm = rows.bm
    halo, tail = rows.halo(width), rows.tail(width)
    nblk = m // bm
    halo_blocks = bm // BF16_ROWS
    in_specs = [pl.BlockSpec((BF16_ROWS, k),
                             lambda i, j: (jnp.maximum(i * halo_blocks - 1, 0), 0)),
                pl.BlockSpec((bm, k), lambda i, j: (i, 0))]
    args = [x_bf, x_bf]
    for arr, off in weights:
        in_specs.append(pl.BlockSpec((k, bn), lambda i, j, o=off // bn: (0, o + j)))
        args.append(arr)
    in_specs += [pl.BlockSpec((None, halo, bn), lambda i, j: (i // rows.bps, 0, j)),
                 pl.BlockSpec((SUBLANES, bn), lambda i, j: (0, j))]
    args += [hist, cw]
    scratch = [pltpu.VMEM((halo + bm, k) if rows.recompute else (BF16_ROWS, LANES), BF16),
               pltpu.VMEM((halo + bm, bn), F32)]
    return pl.pallas_call(
        functools.partial(kernel, rows=rows, halo=halo, tail=tail),
        out_shape=(jax.ShapeDtypeStruct((m, n_cols), out_dtype),
                   jax.ShapeDtypeStruct((nblk, tail, n_cols), F32)),
        grid=(nblk, w_cols // bn),
        in_specs=in_specs,
        out_specs=(pl.BlockSpec((bm, bn), lambda i, j: (i, j)),
                   pl.BlockSpec((None, tail, bn), lambda i, j: (i, 0, j))),
        scratch_shapes=scratch,
        compiler_params=_params("parallel", "arbitrary"),
        name=name,
    )(*args)


def _mixa_kernel(xh_ref, xm_ref, wh_ref, wbg_ref, wcg_ref, hist_ref, cw_ref, y_ref, tail_ref,
                 xcat_ref, buf_ref, *, rows, halo, tail):
    bm = xm_ref.shape[0]
    if rows.recompute:
        lhs = _stage_lhs(xh_ref, xm_ref, xcat_ref, halo)
        u = _dot(lhs, wcg_ref[...]) * _dot(lhs, wh_ref[...])
        top, main = _split_halo(u, hist_ref, rows, halo)
    else:
        lhs = xm_ref[...]
        top, main = hist_ref[...], _dot(lhs, wcg_ref[...]) * _dot(lhs, wh_ref[...])
    conv = _conv_from_buf(buf_ref, top, main, cw_ref[...], CONV_A, halo, rows.stride)
    y_ref[...] = (_dot(xm_ref[...], wbg_ref[...]) * conv).astype(y_ref.dtype)
    tail_ref[...] = main[bm - tail:, :]


def _qkvz_kernel(xh_ref, xm_ref, w_ref, hist_ref, cw_ref, o_ref, tail_ref, xcat_ref, buf_ref,
                 *, rows, halo, tail):
    bm, bn = o_ref.shape
    part = pl.program_id(1) // (D_B // bn)
    if rows.recompute:
        res = _dot(_stage_lhs(xh_ref, xm_ref, xcat_ref, halo), w_ref[...])
        top, main = _split_halo(res, hist_ref, rows, halo)
    else:
        top, main = hist_ref[...], _dot(xm_ref[...], w_ref[...])
    tail_ref[...] = main[bm - tail:, :]

    @pl.when(part == 3)
    def _():
        o_ref[...] = main

    @pl.when(part < 3)
    def _():
        s = _silu(_conv_from_buf(buf_ref, top, main, cw_ref[...], CONV_B, halo, rows.stride))
        nrm = jnp.concatenate([_l2n(s[:, h * DK:(h + 1) * DK]) for h in range(bn // DK)], axis=1)
        scale = jnp.where(part == 0, DK ** -0.5, 1.0)
        o_ref[...] = jnp.where(part < 2, nrm * scale, s)


def _ffn_up_kernel(xh_ref, xm_ref, wg_ref, wu_ref, hist_ref, cw_ref, o_ref, tail_ref,
                   xcat_ref, buf_ref, *, rows, halo, tail):
    bm = xm_ref.shape[0]
    if rows.recompute:
        res = _dot(_stage_lhs(xh_ref, xm_ref, xcat_ref, halo), wg_ref[...])
        top, main = _split_halo(res, hist_ref, rows, halo)
    else:
        top, main = hist_ref[...], _dot(xm_ref[...], wg_ref[...])
    conv = _conv_from_buf(buf_ref, top, main, cw_ref[...], CONV_F, halo, rows.stride)
    o_ref[...] = (_silu(conv) * _dot(xm_ref[...], wu_ref[...])).astype(o_ref.dtype)
    tail_ref[...] = main[bm - tail:, :]


def _neumann(xs, eye, order):
    ps = [eye - x for x in xs]
    xp = xs
    n = 2
    while n < order:
        xp = [_dot(x, x) for x in xp]
        ps = [p + _dot(p, x) for p, x in zip(ps, xp)]
        n *= 2
    return ps


def _unit_lower_inverse(lmats, chunk, blk, order):
    row = lax.broadcasted_iota(jnp.int32, (chunk, chunk), 0)
    col = lax.broadcasted_iota(jnp.int32, (chunk, chunk), 1)
    eye = (row == col).astype(F32)
    if order <= blk:
        return _neumann(lmats, eye, order)
    same = jnp.bitwise_xor(row, col) < blk
    d_inv = _neumann([jnp.where(same, l, 0.0) for l in lmats], eye, blk)
    ms = [_dot(d, jnp.where(same, 0.0, l)) for d, l in zip(d_inv, lmats)]
    m_inv = _neumann(ms, eye, order // blk)
    return [_dot(mi, d) for mi, d in zip(m_inv, d_inv)]


def _delta_kernel(q_ref, k_ref, v_ref, z_ref, bg_ref, alog_ref, dtb_ref, ng_ref, s0_ref,
                  y_ref, sfin_ref, s_scr, *, nseq, chunk, valid, n_chunks, blk, order):
    c = pl.program_id(1)

    @pl.when(c == 0)
    def _():
        s_scr[...] = s0_ref[...]

    row = lax.broadcasted_iota(jnp.int32, (chunk, chunk), 0)
    col = lax.broadcasted_iota(jnp.int32, (chunk, chunk), 1)
    causal = row >= col
    strict = row > col
    tri = causal.astype(F32)
    neg_a = -jnp.exp(alog_ref[...])
    dtb = dtb_ref[...]
    ng = ng_ref[...]

    betas, gcums, gcum_ts = [], [], []
    for s in range(nseq):
        beta = _sigmoid(bg_ref[s, :, 0:LANES])
        g = neg_a * _softplus(bg_ref[s, :, LANES:2 * LANES] + dtb)
        if valid < chunk:
            live = lax.broadcasted_iota(jnp.int32, (chunk, LANES), 0) < valid
            beta = jnp.where(live, beta, 0.0)
            g = jnp.where(live, g, 0.0)
        gcum = _dot_hi(tri, g)
        gpad = jnp.concatenate([gcum, jnp.zeros((LANES - chunk, LANES), F32)], axis=0)
        betas.append(beta)
        gcums.append(gcum)
        gcum_ts.append(gpad.T)

    probs = [(s, h) for s in range(nseq) for h in range(H_B)]
    cols = [slice(h * DK, (h + 1) * DK) for _, h in probs]
    qh = [q_ref[s, :, sl] for (s, _), sl in zip(probs, cols)]
    kh = [k_ref[s, :, sl] for (s, _), sl in zip(probs, cols)]
    vh = [v_ref[s, :, sl] for (s, _), sl in zip(probs, cols)]
    bcol = [betas[s][:, h:h + 1] for s, h in probs]
    gcol = [gcums[s][:, h:h + 1] for s, h in probs]
    glast = [gcums[s][chunk - 1:chunk, h:h + 1] for s, h in probs]
    decay = [jnp.exp(jnp.where(causal, gcums[s][:, h:h + 1] - gcum_ts[s][h:h + 1, 0:chunk], -jnp.inf))
             for s, h in probs]
    eg = [jnp.exp(g) for g in gcol]
    kb = [k * b for k, b in zip(kh, bcol)]

    kq = [_dot_nt(jnp.concatenate([a, q], axis=0), k) for a, q, k in zip(kb, qh, kh)]
    lmat = [jnp.where(strict, m[:chunk] * d, 0.0) for m, d in zip(kq, decay)]
    qk = [m[chunk:] * d for m, d in zip(kq, decay)]
    tmat = _unit_lower_inverse(lmat, chunk, blk, order)
    uw = [_dot(t, jnp.concatenate([v * b, a * e], axis=1))
          for t, v, b, a, e in zip(tmat, vh, bcol, kb, eg)]
    st = [s_scr[s, h] for s, h in probs]
    wq = [_dot(jnp.concatenate([m[:, DV:], q * e], axis=0), s)
          for m, q, e, s in zip(uw, qh, eg, st)]
    v_new = [m[:, :DV] - x[:chunk] for m, x in zip(uw, wq)]
    o = [x[chunk:] + _dot(a, vn) for x, a, vn in zip(wq, qk, v_new)]
    s_new = [s * jnp.exp(gl) + _dot_tn(k * jnp.exp(gl - g), vn)
             for s, gl, k, g, vn in zip(st, glast, kh, gcol, v_new)]
    for (s, h), sl, sn, oo in zip(probs, cols, s_new, o):
        s_scr[s, h] = sn
        on = oo * lax.rsqrt(jnp.mean(oo * oo, axis=-1, keepdims=True) + RMS_EPS) * ng
        y_ref[s, :, sl] = (on * _silu(z_ref[s, :, sl])).astype(y_ref.dtype)

    @pl.when(c == n_chunks - 1)
    def _():
        sfin_ref[...] = s_scr[...]


def _delta(qkvz, bg, alog, dtb, ng, s0_all, layer, nseq, chunk, valid, blk, name):
    nb, seq, _ = qkvz.shape
    n_chunks = seq // chunk
    order = chunk
    while order // 2 >= valid:
        order //= 2

    def part(p):
        return pl.BlockSpec((nseq, chunk, D_B), lambda b, c, o=p: (b, c, o))

    vec = pl.BlockSpec((1, LANES), lambda b, c: (0, 0))
    st = pl.BlockSpec((nseq, H_B, DK, DV), lambda b, c: (b, 0, 0, 0))
    st_in = pl.BlockSpec((None, nseq, H_B, DK, DV), lambda b, c, i=layer: (i, b, 0, 0, 0))
    return pl.pallas_call(
        functools.partial(_delta_kernel, nseq=nseq, chunk=chunk, valid=valid, n_chunks=n_chunks,
                          blk=blk, order=order),
        out_shape=(jax.ShapeDtypeStruct((nb, seq, D_B), BF16),
                   jax.ShapeDtypeStruct((nb, H_B, DK, DV), F32)),
        grid=(nb // nseq, n_chunks),
        in_specs=[part(0), part(1), part(2), part(3),
                  pl.BlockSpec((nseq, chunk, 2 * LANES), lambda b, c: (b, c, 0)),
                  vec, vec, vec, st_in],
        out_specs=(part(0), st),
        scratch_shapes=[pltpu.VMEM((nseq, H_B, DK, DV), F32)],
        compiler_params=_params("parallel", "arbitrary"),
        name=name,
    )(qkvz, qkvz, qkvz, qkvz, bg, alog, dtb, ng, s0_all)


def _mixc_kernel(*refs, steps):
    if steps:
        x_ref, wcu_ref, wcv_ref, g_ref, b_ref, wc_ref, bias_ref, y_ref, vc_ref = refs
    else:
        x_ref, wcu_ref, wcv_ref, g_ref, b_ref, ws_ref, bias_ref, y_ref = refs
    x = x_ref[...]
    bm = x.shape[0]
    vc = _ln(_gelu(_dot(x, wcv_ref[...])), g_ref[...], b_ref[...])
    u = _gelu(_dot(x, wcu_ref[...]))
    if steps:
        nb = bm // steps
        vc_ref[...] = vc
        for t in range(steps):
            acc = None
            for s in range(t + 1):
                term = wc_ref[t * steps + s:t * steps + s + 1, :] * vc[s * nb:(s + 1) * nb, :]
                acc = term if acc is None else acc + term
            mixed = acc + bias_ref[t:t + 1, :]
            y_ref[t * nb:(t + 1) * nb, :] = (u[t * nb:(t + 1) * nb, :] * mixed).astype(y_ref.dtype)
    else:
        n_chunks = bm // CHUNK_C
        row = lax.broadcasted_iota(jnp.int32, (CHUNK_C, CHUNK_C), 0)
        col = lax.broadcasted_iota(jnp.int32, (CHUNK_C, CHUNK_C), 1)
        tri = row >= col
        cg = D_C // G_C
        for g in range(G_C):
            sl = slice(g * cg, (g + 1) * cg)
            wm = jnp.where(tri, ws_ref[g], 0.0)
            vcat = jnp.concatenate([vc[n * CHUNK_C:(n + 1) * CHUNK_C, sl] for n in range(n_chunks)],
                                   axis=1)
            mixed = _dot(wm, vcat)
            for n in range(n_chunks):
                rs = slice(n * CHUNK_C, (n + 1) * CHUNK_C)
                val = u[rs, sl] * (mixed[:, n * cg:(n + 1) * cg] + bias_ref[:, sl])
                y_ref[rs, sl] = val.astype(y_ref.dtype)


def _mixc(x_bf, w_c, g, b, mix_w, bias, bm, steps, name):
    m, k = x_bf.shape
    full = lambda a: pl.BlockSpec(a.shape, lambda i: (0,) * a.ndim)
    row = pl.BlockSpec((bm, D_C), lambda i: (i, 0))
    out_shape = [jax.ShapeDtypeStruct((m, D_C), BF16)]
    out_specs = [row]
    if steps:
        out_shape.append(jax.ShapeDtypeStruct((m, D_C), F32))
        out_specs.append(row)
    return pl.pallas_call(
        functools.partial(_mixc_kernel, steps=steps),
        out_shape=tuple(out_shape),
        grid=(m // bm,),
        in_specs=[pl.BlockSpec((bm, k), lambda i: (i, 0)),
                  pl.BlockSpec((k, D_C), lambda i: (0, 0)),
                  pl.BlockSpec((k, D_C), lambda i: (0, 1)),
                  full(g), full(b), full(mix_w), full(bias)],
        out_specs=tuple(out_specs),
        compiler_params=_params("parallel"),
        name=name,
    )(x_bf, w_c, w_c, g, b, mix_w, bias)


def _merge_kernel(x_ref, ya_ref, yb_ref, yc_ref, g0_ref, g1_ref, g2_ref, wa_ref, wb_ref, wc_ref, o_ref):
    x = x_ref[...]
    m = _sigmoid(_dot(x, g0_ref[...])) * _dot(ya_ref[...], wa_ref[...])
    m = m + _sigmoid(_dot(x, g1_ref[...])) * _dot(yb_ref[...], wb_ref[...])
    m = m + _sigmoid(_dot(x, g2_ref[...])) * _dot(yc_ref[...], wc_ref[...])
    o_ref[...] = m.astype(o_ref.dtype)


def _merge(x_bf, ya, yb, yc, w_c, wa, wb, wc, bm, bn=256):
    m = x_bf.shape[0]
    ys = pl.BlockSpec((bm, D_A), lambda i, j: (i, 0))
    ws = pl.BlockSpec((D_A, bn), lambda i, j: (0, j))

    def gate(branch):
        first = (2 * D_C + branch * D_MODEL) // bn
        return pl.BlockSpec((D_MODEL, bn), lambda i, j, o=first: (0, o + j))

    return pl.pallas_call(
        _merge_kernel,
        out_shape=jax.ShapeDtypeStruct((m, D_MODEL), BF16),
        grid=(m // bm, D_MODEL // bn),
        in_specs=[pl.BlockSpec((bm, D_MODEL), lambda i, j: (i, 0)), ys, ys, ys,
                  gate(0), gate(1), gate(2), ws, ws, ws],
        out_specs=pl.BlockSpec((bm, bn), lambda i, j: (i, j)),
        compiler_params=_params("parallel", "arbitrary"),
        name="merge",
    )(x_bf, ya, yb, yc, w_c, w_c, w_c, wa, wb, wc)


def _o_ln1_kernel(m_ref, w_ref, x_ref, g_ref, b_ref, o_ref, obf_ref):
    y = DEEPNORM_ALPHA * x_ref[...] + _dot(m_ref[...], w_ref[...])
    r = _ln(y, g_ref[...], b_ref[...])
    o_ref[...] = r
    obf_ref[...] = r.astype(BF16)


def _o_ln1(merged, w_o, x, g, b, bm=256):
    m, d = x.shape
    row = pl.BlockSpec((bm, d), lambda i: (i, 0))
    vec = pl.BlockSpec((1, d), lambda i: (0, 0))
    return pl.pallas_call(
        _o_ln1_kernel,
        out_shape=(jax.ShapeDtypeStruct((m, d), F32), jax.ShapeDtypeStruct((m, d), BF16)),
        grid=(m // bm,),
        in_specs=[row, pl.BlockSpec((d, d), lambda i: (0, 0)), row, vec, vec],
        out_specs=(row, row),
        compiler_params=_params("parallel"),
        name="o_ln1",
    )(merged, w_o, x, g.reshape(1, d), b.reshape(1, d))


def _down_ln2_kernel(act_ref, wd_ref, xbf_ref, wpg_ref, p_ref, wpe_ref, xres_ref, g_ref, b_ref,
                     o_ref, obf_ref, acc_ref, *, nt):
    j = pl.program_id(1)
    bn = xres_ref.shape[1]
    ffn = _dot(act_ref[...], wd_ref[...])
    ple = _sigmoid(_dot(xbf_ref[...], wpg_ref[...])) * _dot(p_ref[...], wpe_ref[...])
    acc_ref[j] = DEEPNORM_ALPHA * xres_ref[...] + ffn + ple

    @pl.when(j == nt - 1)
    def _():
        tiles = [acc_ref[t] for t in range(nt)]
        inv_n = 1.0 / (nt * bn)
        mu = sum(jnp.sum(t, axis=-1, keepdims=True) for t in tiles) * inv_n
        cent = [t - mu for t in tiles]
        var = sum(jnp.sum(c * c, axis=-1, keepdims=True) for c in cent) * inv_n
        inv = lax.rsqrt(var + LN_EPS)
        for t, c in enumerate(cent):
            sl = slice(t * bn, (t + 1) * bn)
            r = c * inv * g_ref[:, sl] + b_ref[:, sl]
            o_ref[:, sl] = r
            obf_ref[:, sl] = r.astype(BF16)


def _down_ln2(act, w_down, x1, x1_bf, p, wpg, wpe, g, b, bm, bn=256):
    m, d = x1.shape
    nt = d // bn
    kf = act.shape[1]
    row = pl.BlockSpec((bm, d), lambda i, j: (i, 0))
    vec = pl.BlockSpec((1, d), lambda i, j: (0, 0))
    return pl.pallas_call(
        functools.partial(_down_ln2_kernel, nt=nt),
        out_shape=(jax.ShapeDtypeStruct((m, d), F32), jax.ShapeDtypeStruct((m, d), BF16)),
        grid=(m // bm, nt),
        in_specs=[pl.BlockSpec((bm, kf), lambda i, j: (i, 0)),
                  pl.BlockSpec((kf, bn), lambda i, j: (0, j)),
                  row,
                  pl.BlockSpec((d, bn), lambda i, j: (0, j)),
                  pl.BlockSpec((bm, D_PLE), lambda i, j: (i, 0)),
                  pl.BlockSpec((D_PLE, bn), lambda i, j: (0, j)),
                  pl.BlockSpec((bm, bn), lambda i, j: (i, j)),
                  vec, vec],
        out_specs=(row, row),
        scratch_shapes=[pltpu.VMEM((nt, bm, bn), F32)],
        compiler_params=_params("parallel", "arbitrary"),
        name="down_ln2",
    )(act, w_down, x1_bf, wpg, p, wpe, x1, g.reshape(1, d), b.reshape(1, d))


def _pad_rows(w, rows):
    return jnp.pad(w, ((0, rows - w.shape[0]), (0, 0)))


def _pad_cols(w, cols):
    return jnp.pad(w, ((0, 0),) * (w.ndim - 1) + ((0, cols - w.shape[-1]),))


def _layer_weights(i, w_in, conv_a_w, w_a_out, conv_b_w, a_log, dt_bias, norm_b_g, w_b_out,
                   ln_c_g, ln_c_b, w_s, b_s, w_c_out, w_o, ln1_g, ln1_b,
                   w_up, conv_f_w, w_down, w_pe, w_pg, ln2_g, ln2_b):
    wi = w_in[i]
    w = {}
    w["ab"] = wi[:, :OFF_BETA].astype(BF16)
    w["c"] = wi[:, OFF_CU:].astype(BF16)
    w["small"] = jnp.concatenate([_pad_cols(wi[:, OFF_BETA:OFF_DEC], LANES),
                                  _pad_cols(wi[:, OFF_DEC:OFF_CU], LANES)], axis=1).astype(BF16)
    w["conv_a"] = _pad_rows(conv_a_w[i], SUBLANES)
    w["conv_b"] = _pad_cols(_pad_rows(conv_b_w[i], SUBLANES), D_QKVZ)
    w["conv_f"] = _pad_cols(_pad_rows(conv_f_w[i], SUBLANES), D_FF_PAD)
    w["a_out"] = w_a_out[i].astype(BF16)
    w["b_out"] = w_b_out[i].astype(BF16)
    w["c_out"] = w_c_out[i].astype(BF16)
    w["alog"] = _pad_cols(a_log[i].reshape(1, H_B), LANES)
    w["dtb"] = _pad_cols(dt_bias[i].reshape(1, H_B), LANES)
    w["ng"] = norm_b_g[i].reshape(1, DV)
    w["lncg"] = ln_c_g[i].reshape(1, D_C)
    w["lncb"] = ln_c_b[i].reshape(1, D_C)
    w["ws"] = w_s[i]
    w["bias_c"] = jnp.repeat(b_s[i].T, D_C // G_C, axis=1)
    w["o"] = w_o[i].astype(BF16)
    w["ln1g"], w["ln1b"] = ln1_g[i], ln1_b[i]
    w["up_g"] = _pad_cols(w_up[i][:, :D_FF].astype(BF16), D_FF_PAD)
    w["up_u"] = _pad_cols(w_up[i][:, D_FF:].astype(BF16), D_FF_PAD)
    w["down"] = _pad_rows(w_down[i].astype(BF16), D_FF_PAD)
    w["pe"] = w_pe[i].astype(BF16)
    w["pg"] = w_pg[i].astype(BF16)
    w["ln2g"], w["ln2b"] = ln2_g[i], ln2_b[i]
    return w


def _sample_mix_coeffs(w, steps):
    cg = D_C // G_C
    wc = jnp.transpose(w["ws"][:, :steps, :steps], (1, 2, 0))
    wc = jnp.repeat(wc, cg, axis=2).reshape(steps * steps, D_C)
    return wc, w["bias_c"][:SUBLANES]


def _layer(x, x_bf, p2d, w, rows, hists, delta_fn, mixc_fn):
    hist_a, hist_q, hist_f = hists
    ya, tail_a = _conv_call(_mixa_kernel, rows, CONV_A, x_bf,
                            [(w["ab"], 0), (w["ab"], D_A), (w["ab"], 2 * D_A)], D_A,
                            hist_a, w["conv_a"], D_A, 256, BF16, "mixa")
    qkvz, tail_q = _conv_call(_qkvz_kernel, rows, CONV_B, x_bf, [(w["ab"], OFF_QKV)], D_QKVZ,
                              hist_q, w["conv_b"], D_QKVZ, 512, F32, "qkvz")
    small = _mm(x_bf, w["small"], rows.bm, 2 * LANES, F32, "in_proj_small")
    yb, s_fin = delta_fn(qkvz, small)
    yc, *vc = mixc_fn(x_bf)

    merged = _merge(x_bf, ya, yb, yc, w["c"], w["a_out"], w["b_out"], w["c_out"], rows.bm)
    x1, x1_bf = _o_ln1(merged, w["o"], x, w["ln1g"], w["ln1b"])
    act, tail_f = _conv_call(_ffn_up_kernel, rows, CONV_F, x1_bf,
                             [(w["up_g"], 0), (w["up_u"], 0)], D_FF_PAD,
                             hist_f, w["conv_f"], D_FF_PAD, 512, BF16, "ffn_up")
    x2, x2_bf = _down_ln2(act, w["down"], x1, x1_bf, p2d, w["pg"], w["pe"],
                          w["ln2g"], w["ln2b"], 512)
    return x2, x2_bf, tail_a, tail_q, s_fin, tail_f, vc


def _tm(a):
    return jnp.swapaxes(a, 0, 1)


def _layer_prompt(x, x_bf, p2d, w, nb, seq):
    rows = _Rows(bm=1024, stride=1, bps=seq // 1024, recompute=True)
    m = x.shape[0]
    hists = tuple(jnp.zeros((nb, BF16_ROWS, c), F32) for c in (D_A, D_QKVZ, D_FF_PAD))

    def delta_fn(qkvz, small):
        yb, s_fin = _delta(qkvz.reshape(nb, seq, D_QKVZ), small.reshape(nb, seq, 2 * LANES),
                           w["alog"], w["dtb"], w["ng"], jnp.zeros((1, nb, H_B, DK, DV), F32), 0,
                           1, DELTA_CHUNK, DELTA_CHUNK, SUBLANES, "delta_prompt")
        return yb.reshape(m, D_B), s_fin

    def mixc_fn(xb):
        return _mixc(xb, w["c"], w["lncg"], w["lncb"], w["ws"], w["bias_c"], 512, 0, "mixc_prompt")

    x2, x2_bf, tail_a, tail_q, s_fin, tail_f, _ = _layer(x, x_bf, p2d, w, rows, hists, delta_fn, mixc_fn)
    last = slice(rows.bps - 1, None, rows.bps)
    new_a = tail_a[last, SUBLANES - (CONV_A - 1):, :]
    new_q = tail_q[last, SUBLANES - (CONV_B - 1):, :3 * D_B]
    new_f = tail_f[last, SUBLANES - (CONV_F - 1):, :D_FF]
    return x2, x2_bf, new_a, new_q, s_fin, new_f


def _layer_sample(x, x_bf, p2d, w, st_a, st_q, st_d_all, layer, st_f, nb, steps):
    m = x.shape[0]
    rows = _Rows(bm=m, stride=nb, bps=1, recompute=False)
    chunk = 2 * SUBLANES

    def hist(st, cols):
        h = _tm(st)
        return _pad_cols(h.reshape(1, h.shape[0] * nb, h.shape[2]), cols)

    hists = (hist(st_a, D_A), hist(st_q, D_QKVZ), hist(st_f, D_FF_PAD))

    def to_chunks(a2d):
        a = _tm(a2d.reshape(steps, nb, a2d.shape[1]))
        return jnp.pad(a, ((0, 0), (0, chunk - steps), (0, 0)))

    def delta_fn(qkvz, small):
        yb_b, s_fin = _delta(to_chunks(qkvz), to_chunks(small), w["alog"], w["dtb"], w["ng"],
                             st_d_all, layer, 4, chunk, steps, SUBLANES, "delta_sample")
        return _tm(yb_b[:, :steps]).reshape(m, D_B), s_fin

    def mixc_fn(xb):
        wc, bias = _sample_mix_coeffs(w, steps)
        return _mixc(xb, w["c"], w["lncg"], w["lncb"], wc, bias, m, steps, "mixc_sample")

    x2, x2_bf, tail_a, tail_q, s_fin, tail_f, vc = _layer(x, x_bf, p2d, w, rows, hists, delta_fn, mixc_fn)

    def from_tail(t, cols):
        return _tm(t.reshape(t.shape[1] // nb, nb, t.shape[2]))[:, :, :cols]

    new_v = _tm(vc[0].reshape(steps, nb, D_C))
    return (x2, x2_bf, from_tail(tail_a, D_A), from_tail(tail_q, 3 * D_B), s_fin,
            from_tail(tail_f, D_FF), new_v)


def kernel(x_prompt, x_sample, state_conv_a, state_conv_qkv, state_delta, state_conv_ffn,
           p_prompt, p_sample, ln_in_g, ln_in_b,
           w_in, conv_a_w, w_a_out, conv_b_w, a_log, dt_bias, norm_b_g, w_b_out,
           ln_c_g, ln_c_b, w_s, b_s, w_c_out, w_o, ln1_g, ln1_b,
           w_up, conv_f_w, w_down, w_pe, w_pg, ln2_g, ln2_b):
    nbp, seq, d = x_prompt.shape
    nbs, steps, _ = x_sample.shape
    mp, ms = nbp * seq, nbs * steps

    xp, xp_bf = _ln_in(x_prompt.reshape(mp, d), ln_in_g, ln_in_b, 256, "ln_in")
    xs, xs_bf = _ln_in(_tm(x_sample).reshape(ms, d), ln_in_g, ln_in_b, 256, "ln_in_s")

    outs_p = [[] for _ in range(4)]
    outs_s = [[] for _ in range(5)]
    for i in range(DEPTH):
        w = _layer_weights(i, w_in, conv_a_w, w_a_out, conv_b_w, a_log, dt_bias, norm_b_g, w_b_out,
                           ln_c_g, ln_c_b, w_s, b_s, w_c_out, w_o, ln1_g, ln1_b,
                           w_up, conv_f_w, w_down, w_pe, w_pg, ln2_g, ln2_b)
        xp, xp_bf, *st_p = _layer_prompt(xp, xp_bf, p_prompt[i].reshape(mp, D_PLE), w, nbp, seq)
        for acc, val in zip(outs_p, st_p):
            acc.append(val)
        xs, xs_bf, *st_s = _layer_sample(xs, xs_bf, _tm(p_sample[i]).reshape(ms, D_PLE), w,
                                         state_conv_a[i], state_conv_qkv[i], state_delta, i,
                                         state_conv_ffn[i], nbs, steps)
        for acc, val in zip(outs_s, st_s):
            acc.append(val)

    y_prompt = xp.reshape(nbp, seq, d)
    y_sample = _tm(xs.reshape(steps, nbs, d))
    return (y_prompt, y_sample,
            jnp.stack(outs_p[0]), jnp.stack(outs_p[1]), jnp.stack(outs_p[2]), jnp.stack(outs_p[3]),
            jnp.stack(outs_s[0]), jnp.stack(outs_s[1]), jnp.stack(outs_s[2]), jnp.stack(outs_s[3]),
            jnp.stack(outs_s[4]))
```

```python
import functools
import math
from typing import NamedTuple

import jax
import jax.numpy as jnp
from jax import lax
from jax.experimental import pallas as pl
from jax.experimental.pallas import tpu as pltpu

F32 = jnp.float32
BF16 = jnp.bfloat16

D_MODEL = 2048
DEPTH = 2
D_PLE = 256
D_A = D_MODEL // 2
CONV_A = 3
H_B = 8
DK = 128
DV = 128
D_B = H_B * DV
CONV_B = 4
PROMPT_CHUNK = 128
D_C = D_MODEL // 2
CHUNK_C = 128
G_C = 8
D_FF = ((8 * D_MODEL // 3 + 127) // 128) * 128
CONV_F = 3
N_BRANCH = 3
DEEPNORM_ALPHA = (2 * DEPTH) ** 0.25
LN_EPS = 1e-5
RMS_EPS = 1e-6

LANES = 128
SUBLANES = 8
BF16_ROWS = 16
MXU_COLS = 256
D_FF_PAD = 5632
D_QKVZ = 4 * D_B
VMEM_LIMIT = 58 * 1024 * 1024

OFF_QKV = 3 * D_A
OFF_BETA = OFF_QKV + 4 * D_B
OFF_DEC = OFF_BETA + H_B
OFF_CU = OFF_DEC + H_B


class _Rows(NamedTuple):
    bm: int
    stride: int
    bps: int
    recompute: bool

    def halo(self, width):
        return BF16_ROWS if self.recompute else (width - 1) * self.stride

    def tail(self, width):
        return SUBLANES if self.recompute else (width - 1) * self.stride


def _params(*sem):
    return pltpu.CompilerParams(dimension_semantics=sem, vmem_limit_bytes=VMEM_LIMIT)


def _sigmoid(x):
    return 1.0 / (1.0 + jnp.exp(-x))


def _silu(x):
    return x * _sigmoid(x)


def _gelu(x):
    c = math.sqrt(2.0 / math.pi)
    return 0.5 * x * (1.0 + jnp.tanh(c * (x + 0.044715 * (x * x * x))))


def _softplus(x):
    return jnp.maximum(x, 0.0) + jnp.log1p(jnp.exp(-jnp.abs(x)))


def _ln(x, g, b):
    mu = jnp.mean(x, axis=-1, keepdims=True)
    xc = x - mu
    var = jnp.mean(xc * xc, axis=-1, keepdims=True)
    return xc * lax.rsqrt(var + LN_EPS) * g + b


def _l2n(t):
    return t * lax.rsqrt(jnp.sum(t * t, axis=-1, keepdims=True) + RMS_EPS)


def _dot(a, b):
    return jnp.dot(a.astype(BF16), b.astype(BF16), preferred_element_type=F32)


def _dot_nt(a, b):
    return lax.dot_general(a.astype(BF16), b.astype(BF16), (((1,), (1,)), ((), ())),
                           preferred_element_type=F32)


def _dot_tn(a, b):
    return lax.dot_general(a.astype(BF16), b.astype(BF16), (((0,), (0,)), ((), ())),
                           preferred_element_type=F32)


def _dot_hi(a, b):
    return jnp.dot(a, b, precision=lax.Precision.HIGHEST, preferred_element_type=F32)


def _mm_kernel(x_ref, w_ref, o_ref):
    o_ref[...] = _dot(x_ref[...], w_ref[...]).astype(o_ref.dtype)


def _mm(x, w, layer, bm, bn, out_dtype, name):
    m, k = x.shape
    n = w.shape[2]
    assert m % bm == 0 and n % bn == 0
    return pl.pallas_call(
        _mm_kernel,
        out_shape=jax.ShapeDtypeStruct((m, n), out_dtype),
        grid=(m // bm, n // bn),
        in_specs=[pl.BlockSpec((bm, k), lambda i, j: (i, 0)),
                  pl.BlockSpec((None, k, bn), lambda i, j, l=layer: (l, 0, j))],
        out_specs=pl.BlockSpec((bm, bn), lambda i, j: (i, j)),
        compiler_params=_params("parallel", "arbitrary"),
        name=name,
    )(x, w)


def _ln_in_kernel(x_ref, g_ref, b_ref, o_ref, obf_ref):
    r = _ln(x_ref[...], g_ref[...], b_ref[...])
    o_ref[...] = r
    obf_ref[...] = r.astype(BF16)


def _ln_in(x, g, b, bm, name):
    m, d = x.shape
    row = pl.BlockSpec((bm, d), lambda i: (i, 0))
    vec = pl.BlockSpec((1, d), lambda i: (0, 0))
    return pl.pallas_call(
        _ln_in_kernel,
        out_shape=(jax.ShapeDtypeStruct((m, d), F32), jax.ShapeDtypeStruct((m, d), BF16)),
        grid=(m // bm,),
        in_specs=[row, vec, vec],
        out_specs=(row, row),
        compiler_params=_params("parallel"),
        name=name,
    )(x, g.reshape(1, d), b.reshape(1, d))


def _stage_lhs(xh_ref, xm_ref, xcat_ref, halo):
    @pl.when(pl.program_id(1) == 0)
    def _():
        xcat_ref[0:halo, :] = xh_ref[...]
        xcat_ref[halo:, :] = xm_ref[...]
    return xcat_ref[...]


def _split_halo(res, hist_ref, rows, halo):
    is_start = (pl.program_id(0) % rows.bps) == 0
    return jnp.where(is_start, hist_ref[...], res[:halo]), res[halo:]


def _conv_time(top, main, cw, width, halo, stride):
    bm = main.shape[0]
    full = jnp.concatenate([top, main], axis=0)
    out = None
    for j in range(width):
        back = (width - 1 - j) * stride
        if back == 0:
            shifted = main
        elif back % SUBLANES == 0:
            shifted = full[halo - back:halo - back + bm, :]
        else:
            shifted = pltpu.roll(full, back, axis=0)[halo:, :]
        term = shifted * cw[j:j + 1, :]
        out = term if out is None else out + term
    return out


def _conv_call(kernel, rows, width, x_bf, weights, layer, w_cols, hist, cw, n_cols, bn, out_dtype,
               name):
    m, k = x_bf.shape
    bm = rows.bm
    halo, tail = rows.halo(width), rows.tail(width)
    nblk = m // bm
    halo_blocks = bm // BF16_ROWS
    in_specs = [pl.BlockSpec((BF16_ROWS, k),
                             lambda i, j: (jnp.maximum(i * halo_blocks - 1, 0), 0)),
                pl.BlockSpec((bm, k), lambda i, j: (i, 0))]
    args = [x_bf, x_bf]
    for arr, off in weights:
        in_specs.append(pl.BlockSpec((None, k, bn),
                                     lambda i, j, o=off // bn, l=layer: (l, 0, o + j)))
        args.append(arr)
    in_specs += [pl.BlockSpec((None, halo, bn), lambda i, j: (i // rows.bps, 0, j)),
                 pl.BlockSpec((SUBLANES, bn), lambda i, j: (0, j))]
    args += [hist, cw]
    scratch = [pltpu.VMEM((halo + bm, k) if rows.recompute else (BF16_ROWS, LANES), BF16)]
    return pl.pallas_call(
        functools.partial(kernel, rows=rows, halo=halo, tail=tail),
        out_shape=(jax.ShapeDtypeStruct((m, n_cols), out_dtype),
                   jax.ShapeDtypeStruct((nblk, tail, n_cols), F32)),
        grid=(nblk, w_cols // bn),
        in_specs=in_specs,
        out_specs=(pl.BlockSpec((bm, bn), lambda i, j: (i, j)),
                   pl.BlockSpec((None, tail, bn), lambda i, j: (i, 0, j))),
        scratch_shapes=scratch,
        compiler_params=_params("parallel", "arbitrary"),
        name=name,
    )(*args)


def _mixa_kernel(xh_ref, xm_ref, wh_ref, wbg_ref, wcg_ref, hist_ref, cw_ref, y_ref, tail_ref,
                 xcat_ref, *, rows, halo, tail):
    bm = xm_ref.shape[0]
    if rows.recompute:
        lhs = _stage_lhs(xh_ref, xm_ref, xcat_ref, halo)
        u = _dot(lhs, wcg_ref[...]) * _dot(lhs, wh_ref[...])
        top, main = _split_halo(u, hist_ref, rows, halo)
    else:
        lhs = xm_ref[...]
        top, main = hist_ref[...], _dot(lhs, wcg_ref[...]) * _dot(lhs, wh_ref[...])
    conv = _conv_time(top, main, cw_ref[...], CONV_A, halo, rows.stride)
    y_ref[...] = (_dot(xm_ref[...], wbg_ref[...]) * conv).astype(y_ref.dtype)
    tail_ref[...] = main[bm - tail:, :]


def _qkvz_kernel(xh_ref, xm_ref, w_ref, hist_ref, cw_ref, o_ref, tail_ref, xcat_ref,
                 *, rows, halo, tail):
    bm, bn = o_ref.shape
    assert bn == D_B
    part = pl.program_id(1)
    if rows.recompute:
        lhs = _stage_lhs(xh_ref, xm_ref, xcat_ref, halo)

    @pl.when(part == 3)
    def _():
        main = _dot(xm_ref[...], w_ref[...])
        o_ref[...] = main
        tail_ref[...] = main[bm - tail:, :]

    def project(c):
        cs = slice(c * MXU_COLS, (c + 1) * MXU_COLS)
        if rows.recompute:
            return _split_halo(_dot(lhs, w_ref[:, cs]), hist_ref.at[:, cs], rows, halo)
        return hist_ref[:, cs], _dot(xm_ref[...], w_ref[:, cs])

    def conv_part(norm_scale):
        n_sub = bn // MXU_COLS
        nxt = project(0)
        for c in range(n_sub):
            cs = slice(c * MXU_COLS, (c + 1) * MXU_COLS)
            top, main = nxt
            if c + 1 < n_sub:
                nxt = project(c + 1)
            tail_ref[:, cs] = main[bm - tail:, :]
            s = _silu(_conv_time(top, main, cw_ref[:, cs], CONV_B, halo, rows.stride))
            for h in range(MXU_COLS // DK):
                sh = s[:, h * DK:(h + 1) * DK]
                if norm_scale is not None:
                    inv = lax.rsqrt(jnp.sum(sh * sh, axis=-1, keepdims=True) + RMS_EPS)
                    sh = sh * (inv * norm_scale) if norm_scale != 1.0 else sh * inv
                o_ref[:, c * MXU_COLS + h * DK:c * MXU_COLS + (h + 1) * DK] = sh

    for p, norm_scale in enumerate((DK ** -0.5, 1.0, None)):
        pl.when(part == p)(functools.partial(conv_part, norm_scale))


def _ffn_up_kernel(xh_ref, xm_ref, wg_ref, wu_ref, hist_ref, cw_ref, o_ref, tail_ref,
                   xcat_ref, *, rows, halo, tail):
    bm = xm_ref.shape[0]
    if rows.recompute:
        res = _dot(_stage_lhs(xh_ref, xm_ref, xcat_ref, halo), wg_ref[...])
        top, main = _split_halo(res, hist_ref, rows, halo)
    else:
        top, main = hist_ref[...], _dot(xm_ref[...], wg_ref[...])
    conv = _conv_time(top, main, cw_ref[...], CONV_F, halo, rows.stride)
    o_ref[...] = (_silu(conv) * _dot(xm_ref[...], wu_ref[...])).astype(o_ref.dtype)
    tail_ref[...] = main[bm - tail:, :]


def _neumann(xs, eye, order):
    ps = [eye - x for x in xs]
    xp = xs
    n = 2
    while n < order:
        xp = [_dot(x, x) for x in xp]
        ps = [p + _dot(p, x) for p, x in zip(ps, xp)]
        n *= 2
    return ps


def _unit_lower_inverse(lmats, chunk, blk, order):
    row = lax.broadcasted_iota(jnp.int32, (chunk, chunk), 0)
    col = lax.broadcasted_iota(jnp.int32, (chunk, chunk), 1)
    eye = (row == col).astype(F32)
    if order <= blk:
        return _neumann(lmats, eye, order)
    same = jnp.bitwise_xor(row, col) < blk
    d_inv = _neumann([jnp.where(same, l, 0.0) for l in lmats], eye, blk)
    ms = [_dot(d, jnp.where(same, 0.0, l)) for d, l in zip(d_inv, lmats)]
    m_inv = _neumann(ms, eye, order // blk)
    return [_dot(mi, d) for mi, d in zip(m_inv, d_inv)]


def _delta_kernel(q_ref, k_ref, v_ref, z_ref, bg_ref, alog_ref, dtb_ref, ng_ref, s0_ref,
                  y_ref, sfin_ref, s_scr, *, nseq, chunk, valid, n_chunks, blk, order):
    c = pl.program_id(1)

    @pl.when(c == 0)
    def _():
        s_scr[...] = s0_ref[...]

    row = lax.broadcasted_iota(jnp.int32, (chunk, chunk), 0)
    col = lax.broadcasted_iota(jnp.int32, (chunk, chunk), 1)
    causal = row >= col
    strict = row > col
    tri = causal.astype(F32)
    neg_a = -jnp.exp(alog_ref[...])
    dtb = dtb_ref[...]
    ng = ng_ref[...]

    betas, gcums, gcum_ts = [], [], []
    for s in range(nseq):
        beta = _sigmoid(bg_ref[s, :, 0:LANES])
        g = neg_a * _softplus(bg_ref[s, :, LANES:2 * LANES] + dtb)
        if valid < chunk:
            live = lax.broadcasted_iota(jnp.int32, (chunk, LANES), 0) < valid
            beta = jnp.where(live, beta, 0.0)
            g = jnp.where(live, g, 0.0)
        gcum = _dot_hi(tri, g)
        gpad = gcum
        if chunk < LANES:
            gpad = jnp.concatenate([gcum, jnp.zeros((LANES - chunk, LANES), F32)], axis=0)
        betas.append(beta)
        gcums.append(gcum)
        gcum_ts.append(gpad.T)

    probs = [(s, h) for s in range(nseq) for h in range(H_B)]
    cols = [slice(h * DK, (h + 1) * DK) for _, h in probs]
    qh = [q_ref[s, :, sl] for (s, _), sl in zip(probs, cols)]
    kh = [k_ref[s, :, sl] for (s, _), sl in zip(probs, cols)]
    vh = [v_ref[s, :, sl] for (s, _), sl in zip(probs, cols)]
    bcol = [betas[s][:, h:h + 1] for s, h in probs]
    gcol = [gcums[s][:, h:h + 1] for s, h in probs]
    glast = [gcums[s][chunk - 1:chunk, h:h + 1] for s, h in probs]
    decay = [jnp.exp(jnp.where(causal, gcums[s][:, h:h + 1] - gcum_ts[s][h:h + 1, 0:chunk], -jnp.inf))
             for s, h in probs]
    eg = [jnp.exp(g) for g in gcol]
    kb = [k * b for k, b in zip(kh, bcol)]

    kq = [_dot_nt(jnp.concatenate([a, q], axis=0), k) for a, q, k in zip(kb, qh, kh)]
    lmat = [jnp.where(strict, m[:chunk] * d, 0.0) for m, d in zip(kq, decay)]
    qk = [m[chunk:] * d for m, d in zip(kq, decay)]
    tmat = _unit_lower_inverse(lmat, chunk, blk, order)
    uw = [_dot(t, jnp.concatenate([v * b, a * e], axis=1))
          for t, v, b, a, e in zip(tmat, vh, bcol, kb, eg)]
    st = [s_scr[s, h] for s, h in probs]
    wq = [_dot(jnp.concatenate([m[:, DV:], q * e], axis=0), s)
          for m, q, e, s in zip(uw, qh, eg, st)]
    v_new = [m[:, :DV] - x[:chunk] for m, x in zip(uw, wq)]
    o = [x[chunk:] + _dot(a, vn) for x, a, vn in zip(wq, qk, v_new)]
    s_new = [s * jnp.exp(gl) + _dot_tn(k * jnp.exp(gl - g), vn)
             for s, gl, k, g, vn in zip(st, glast, kh, gcol, v_new)]
    for (s, h), sl, sn, oo in zip(probs, cols, s_new, o):
        s_scr[s, h] = sn
        on = oo * lax.rsqrt(jnp.mean(oo * oo, axis=-1, keepdims=True) + RMS_EPS) * ng
        y_ref[s, :, sl] = (on * _silu(z_ref[s, :, sl])).astype(y_ref.dtype)

    @pl.when(c == n_chunks - 1)
    def _():
        sfin_ref[...] = s_scr[...]


def _delta(qkvz, bg, alog, dtb, ng, s0_all, layer, nseq, chunk, valid, blk, name):
    nb, seq, _ = qkvz.shape
    n_chunks = seq // chunk
    order = chunk
    while order // 2 >= valid:
        order //= 2

    def part(p):
        return pl.BlockSpec((nseq, chunk, D_B), lambda b, c, o=p: (b, c, o))

    vec = pl.BlockSpec((1, LANES), lambda b, c: (0, 0))
    st = pl.BlockSpec((nseq, H_B, DK, DV), lambda b, c: (b, 0, 0, 0))
    st_in = pl.BlockSpec((None, nseq, H_B, DK, DV), lambda b, c, i=layer: (i, b, 0, 0, 0))
    return pl.pallas_call(
        functools.partial(_delta_kernel, nseq=nseq, chunk=chunk, valid=valid, n_chunks=n_chunks,
                          blk=blk, order=order),
        out_shape=(jax.ShapeDtypeStruct((nb, seq, D_B), BF16),
                   jax.ShapeDtypeStruct((nb, H_B, DK, DV), F32)),
        grid=(nb // nseq, n_chunks),
        in_specs=[part(0), part(1), part(2), part(3),
                  pl.BlockSpec((nseq, chunk, 2 * LANES), lambda b, c: (b, c, 0)),
                  vec, vec, vec, st_in],
        out_specs=(part(0), st),
        scratch_shapes=[pltpu.VMEM((nseq, H_B, DK, DV), F32)],
        compiler_params=_params("parallel", "arbitrary"),
        name=name,
    )(qkvz, qkvz, qkvz, qkvz, bg, alog, dtb, ng, s0_all)


def _mixc_kernel(*refs, steps):
    if steps:
        x_ref, wcu_ref, wcv_ref, g_ref, b_ref, wc_ref, bias_ref, y_ref, vc_ref = refs
    else:
        x_ref, wcu_ref, wcv_ref, g_ref, b_ref, ws_ref, bias_ref, y_ref = refs
    x = x_ref[...]
    bm = x.shape[0]
    vc = _ln(_gelu(_dot(x, wcv_ref[...])), g_ref[...], b_ref[...])
    u = _gelu(_dot(x, wcu_ref[...]))
    if steps:
        nb = bm // steps
        vc_ref[...] = vc
        for t in range(steps):
            acc = None
            for s in range(t + 1):
                term = wc_ref[t * steps + s:t * steps + s + 1, :] * vc[s * nb:(s + 1) * nb, :]
                acc = term if acc is None else acc + term
            mixed = acc + bias_ref[t:t + 1, :]
            y_ref[t * nb:(t + 1) * nb, :] = (u[t * nb:(t + 1) * nb, :] * mixed).astype(y_ref.dtype)
    else:
        n_chunks = bm // CHUNK_C
        row = lax.broadcasted_iota(jnp.int32, (CHUNK_C, CHUNK_C), 0)
        col = lax.broadcasted_iota(jnp.int32, (CHUNK_C, CHUNK_C), 1)
        tri = row >= col
        cg = D_C // G_C
        for g in range(G_C):
            sl = slice(g * cg, (g + 1) * cg)
            wm = jnp.where(tri, ws_ref[g], 0.0)
            vcat = jnp.concatenate([vc[n * CHUNK_C:(n + 1) * CHUNK_C, sl] for n in range(n_chunks)],
                                   axis=1)
            mixed = _dot(wm, vcat)
            for n in range(n_chunks):
                rs = slice(n * CHUNK_C, (n + 1) * CHUNK_C)
                val = u[rs, sl] * (mixed[:, n * cg:(n + 1) * cg] + bias_ref[:, sl])
                y_ref[rs, sl] = val.astype(y_ref.dtype)


def _mixc(x_bf, w_c, layer, g, b, mix_w, bias, bm, steps, name):
    m, k = x_bf.shape
    full = lambda a: pl.BlockSpec(a.shape, lambda i: (0,) * a.ndim)
    row = pl.BlockSpec((bm, D_C), lambda i: (i, 0))
    out_shape = [jax.ShapeDtypeStruct((m, D_C), BF16)]
    out_specs = [row]
    if steps:
        out_shape.append(jax.ShapeDtypeStruct((m, D_C), F32))
        out_specs.append(row)
    return pl.pallas_call(
        functools.partial(_mixc_kernel, steps=steps),
        out_shape=tuple(out_shape),
        grid=(m // bm,),
        in_specs=[pl.BlockSpec((bm, k), lambda i: (i, 0)),
                  pl.BlockSpec((None, k, D_C), lambda i, l=layer: (l, 0, 0)),
                  pl.BlockSpec((None, k, D_C), lambda i, l=layer: (l, 0, 1)),
                  full(g), full(b), full(mix_w), full(bias)],
        out_specs=tuple(out_specs),
        compiler_params=_params("parallel"),
        name=name,
    )(x_bf, w_c, w_c, g, b, mix_w, bias)


def _merge_kernel(x_ref, ya_ref, yb_ref, yc_ref, g0_ref, g1_ref, g2_ref, wa_ref, wb_ref, wc_ref, o_ref):
    x = x_ref[...]
    m = _sigmoid(_dot(x, g0_ref[...])) * _dot(ya_ref[...], wa_ref[...])
    m = m + _sigmoid(_dot(x, g1_ref[...])) * _dot(yb_ref[...], wb_ref[...])
    m = m + _sigmoid(_dot(x, g2_ref[...])) * _dot(yc_ref[...], wc_ref[...])
    o_ref[...] = m.astype(o_ref.dtype)


def _merge(x_bf, ya, yb, yc, w_c, wa, wb, wc, layer, bm, bn=256):
    m = x_bf.shape[0]
    ys = pl.BlockSpec((bm, D_A), lambda i, j: (i, 0))
    ws = pl.BlockSpec((None, D_A, bn), lambda i, j, l=layer: (l, 0, j))

    def gate(branch):
        first = (2 * D_C + branch * D_MODEL) // bn
        return pl.BlockSpec((None, D_MODEL, bn), lambda i, j, o=first, l=layer: (l, 0, o + j))

    return pl.pallas_call(
        _merge_kernel,
        out_shape=jax.ShapeDtypeStruct((m, D_MODEL), BF16),
        grid=(m // bm, D_MODEL // bn),
        in_specs=[pl.BlockSpec((bm, D_MODEL), lambda i, j: (i, 0)), ys, ys, ys,
                  gate(0), gate(1), gate(2), ws, ws, ws],
        out_specs=pl.BlockSpec((bm, bn), lambda i, j: (i, j)),
        compiler_params=_params("parallel", "arbitrary"),
        name="merge",
    )(x_bf, ya, yb, yc, w_c, w_c, w_c, wa, wb, wc)


def _o_ln1_kernel(m_ref, w_ref, x_ref, g_ref, b_ref, o_ref, obf_ref):
    y = DEEPNORM_ALPHA * x_ref[...] + _dot(m_ref[...], w_ref[...])
    r = _ln(y, g_ref[...], b_ref[...])
    o_ref[...] = r
    obf_ref[...] = r.astype(BF16)


def _o_ln1(merged, w_o, layer, x, g, b, bm=256):
    m, d = x.shape
    row = pl.BlockSpec((bm, d), lambda i: (i, 0))
    vec = pl.BlockSpec((1, d), lambda i: (0, 0))
    return pl.pallas_call(
        _o_ln1_kernel,
        out_shape=(jax.ShapeDtypeStruct((m, d), F32), jax.ShapeDtypeStruct((m, d), BF16)),
        grid=(m // bm,),
        in_specs=[row, pl.BlockSpec((None, d, d), lambda i, l=layer: (l, 0, 0)), row, vec, vec],
        out_specs=(row, row),
        compiler_params=_params("parallel"),
        name="o_ln1",
    )(merged, w_o, x, g.reshape(1, d), b.reshape(1, d))


def _down_ln2_kernel(act_ref, wd_ref, xbf_ref, wpg_ref, p_ref, wpe_ref, xres_ref, g_ref, b_ref,
                     o_ref, obf_ref, acc_ref, *, nt):
    j = pl.program_id(1)
    bn = xres_ref.shape[1]
    ffn = _dot(act_ref[...], wd_ref[...])
    ple = _sigmoid(_dot(xbf_ref[...], wpg_ref[...])) * _dot(p_ref[...], wpe_ref[...])
    acc_ref[j] = DEEPNORM_ALPHA * xres_ref[...] + ffn + ple

    @pl.when(j == nt - 1)
    def _():
        tiles = [acc_ref[t] for t in range(nt)]
        inv_n = 1.0 / (nt * bn)
        mu = sum(jnp.sum(t, axis=-1, keepdims=True) for t in tiles) * inv_n
        cent = [t - mu for t in tiles]
        var = sum(jnp.sum(c * c, axis=-1, keepdims=True) for c in cent) * inv_n
        inv = lax.rsqrt(var + LN_EPS)
        for t, c in enumerate(cent):
            sl = slice(t * bn, (t + 1) * bn)
            r = c * inv * g_ref[:, sl] + b_ref[:, sl]
            o_ref[:, sl] = r
            obf_ref[:, sl] = r.astype(BF16)


def _down_ln2(act, w_down, x1, x1_bf, p, wpg, wpe, layer, g, b, bm, bn=512):
    m, d = x1.shape
    nt = d // bn
    kf = act.shape[1]
    row = pl.BlockSpec((bm, d), lambda i, j: (i, 0))
    vec = pl.BlockSpec((1, d), lambda i, j: (0, 0))
    return pl.pallas_call(
        functools.partial(_down_ln2_kernel, nt=nt),
        out_shape=(jax.ShapeDtypeStruct((m, d), F32), jax.ShapeDtypeStruct((m, d), BF16)),
        grid=(m // bm, nt),
        in_specs=[pl.BlockSpec((bm, kf), lambda i, j: (i, 0)),
                  pl.BlockSpec((None, kf, bn), lambda i, j, l=layer: (l, 0, j)),
                  row,
                  pl.BlockSpec((None, d, bn), lambda i, j, l=layer: (l, 0, j)),
                  pl.BlockSpec((bm, D_PLE), lambda i, j: (i, 0)),
                  pl.BlockSpec((None, D_PLE, bn), lambda i, j, l=layer: (l, 0, j)),
                  pl.BlockSpec((bm, bn), lambda i, j: (i, j)),
                  vec, vec],
        out_specs=(row, row),
        scratch_shapes=[pltpu.VMEM((nt, bm, bn), F32)],
        compiler_params=_params("parallel", "arbitrary"),
        name="down_ln2",
    )(act, w_down, x1_bf, wpg, p, wpe, x1, g.reshape(1, d), b.reshape(1, d))


def _pad_rows(w, rows):
    return jnp.pad(w, ((0, rows - w.shape[0]), (0, 0)))


def _pad_cols(w, cols):
    return jnp.pad(w, ((0, 0),) * (w.ndim - 1) + ((0, cols - w.shape[-1]),))


def _layer_weights(i, big, conv_a_w, conv_b_w, a_log, dt_bias, norm_b_g, ln_c_g, ln_c_b, w_s, b_s,
                   ln1_g, ln1_b, conv_f_w, ln2_g, ln2_b):
    w = dict(big)
    w["layer"] = i
    w["conv_a"] = _pad_rows(conv_a_w[i], SUBLANES)
    w["conv_b"] = _pad_cols(_pad_rows(conv_b_w[i], SUBLANES), D_QKVZ)
    w["conv_f"] = _pad_cols(_pad_rows(conv_f_w[i], SUBLANES), D_FF_PAD)
    w["alog"] = _pad_cols(a_log[i].reshape(1, H_B), LANES)
    w["dtb"] = _pad_cols(dt_bias[i].reshape(1, H_B), LANES)
    w["ng"] = norm_b_g[i].reshape(1, DV)
    w["lncg"] = ln_c_g[i].reshape(1, D_C)
    w["lncb"] = ln_c_b[i].reshape(1, D_C)
    w["ws"] = w_s[i]
    w["bias_c"] = jnp.repeat(b_s[i].T, D_C // G_C, axis=1)
    w["ln1g"], w["ln1b"] = ln1_g[i], ln1_b[i]
    w["ln2g"], w["ln2b"] = ln2_g[i], ln2_b[i]
    return w


def _stacked_weights(w_in, w_a_out, w_b_out, w_c_out, w_o, w_up, w_down, w_pe, w_pg):
    big = {}
    big["ab"] = w_in[:, :, :OFF_BETA].astype(BF16)
    big["c"] = w_in[:, :, OFF_CU:].astype(BF16)
    big["small"] = jnp.concatenate([_pad_cols(w_in[:, :, OFF_BETA:OFF_DEC], LANES),
                                    _pad_cols(w_in[:, :, OFF_DEC:OFF_CU], LANES)],
                                   axis=2).astype(BF16)
    big["a_out"] = w_a_out.astype(BF16)
    big["b_out"] = w_b_out.astype(BF16)
    big["c_out"] = w_c_out.astype(BF16)
    big["o"] = w_o.astype(BF16)
    big["up_g"] = _pad_cols(w_up[:, :, :D_FF].astype(BF16), D_FF_PAD)
    big["up_u"] = _pad_cols(w_up[:, :, D_FF:].astype(BF16), D_FF_PAD)
    big["down"] = jnp.pad(w_down.astype(BF16), ((0, 0), (0, D_FF_PAD - D_FF), (0, 0)))
    big["pe"] = w_pe.astype(BF16)
    big["pg"] = w_pg.astype(BF16)
    return big


def _sample_mix_coeffs(w, steps):
    cg = D_C // G_C
    wc = jnp.transpose(w["ws"][:, :steps, :steps], (1, 2, 0))
    wc = jnp.repeat(wc, cg, axis=2).reshape(steps * steps, D_C)
    return wc, w["bias_c"][:SUBLANES]


def _layer(x, x_bf, p2d, w, rows, hists, delta_fn, mixc_fn):
    hist_a, hist_q, hist_f = hists
    layer = w["layer"]
    ya, tail_a = _conv_call(_mixa_kernel, rows, CONV_A, x_bf,
                            [(w["ab"], 0), (w["ab"], D_A), (w["ab"], 2 * D_A)], layer, D_A,
                            hist_a, w["conv_a"], D_A, 256, BF16, "mixa")
    qkvz, tail_q = _conv_call(_qkvz_kernel, rows, CONV_B, x_bf, [(w["ab"], OFF_QKV)], layer, D_QKVZ,
                              hist_q, w["conv_b"], D_QKVZ, 1024, F32, "qkvz")
    small = _mm(x_bf, w["small"], layer, rows.bm, 2 * LANES, F32, "in_proj_small")
    yb, s_fin = delta_fn(qkvz, small)
    yc, *vc = mixc_fn(x_bf)

    merged = _merge(x_bf, ya, yb, yc, w["c"], w["a_out"], w["b_out"], w["c_out"], layer, rows.bm)
    x1, x1_bf = _o_ln1(merged, w["o"], layer, x, w["ln1g"], w["ln1b"])
    act, tail_f = _conv_call(_ffn_up_kernel, rows, CONV_F, x1_bf,
                             [(w["up_g"], 0), (w["up_u"], 0)], layer, D_FF_PAD,
                             hist_f, w["conv_f"], D_FF_PAD, 512, BF16, "ffn_up")
    x2, x2_bf = _down_ln2(act, w["down"], x1, x1_bf, p2d, w["pg"], w["pe"], layer,
                          w["ln2g"], w["ln2b"], 512)
    return x2, x2_bf, tail_a, tail_q, s_fin, tail_f, vc


def _tm(a):
    return jnp.swapaxes(a, 0, 1)


def _layer_prompt(x, x_bf, p2d, w, nb, seq):
    rows = _Rows(bm=1024, stride=1, bps=seq // 1024, recompute=True)
    m = x.shape[0]
    hists = tuple(jnp.zeros((nb, BF16_ROWS, c), F32) for c in (D_A, D_QKVZ, D_FF_PAD))

    def delta_fn(qkvz, small):
        yb, s_fin = _delta(qkvz.reshape(nb, seq, D_QKVZ), small.reshape(nb, seq, 2 * LANES),
                           w["alog"], w["dtb"], w["ng"], jnp.zeros((1, nb, H_B, DK, DV), F32), 0,
                           1, PROMPT_CHUNK, PROMPT_CHUNK, SUBLANES, "delta_prompt")
        return yb.reshape(m, D_B), s_fin

    def mixc_fn(xb):
        return _mixc(xb, w["c"], w["layer"], w["lncg"], w["lncb"], w["ws"], w["bias_c"], 512, 0,
                     "mixc_prompt")

    x2, x2_bf, tail_a, tail_q, s_fin, tail_f, _ = _layer(x, x_bf, p2d, w, rows, hists, delta_fn, mixc_fn)
    last = slice(rows.bps - 1, None, rows.bps)
    new_a = tail_a[last, SUBLANES - (CONV_A - 1):, :]
    new_q = tail_q[last, SUBLANES - (CONV_B - 1):, :3 * D_B]
    new_f = tail_f[last, SUBLANES - (CONV_F - 1):, :D_FF]
    return x2, x2_bf, new_a, new_q, s_fin, new_f


def _layer_sample(x, x_bf, p2d, w, st_a, st_q, st_d_all, layer, st_f, nb, steps):
    m = x.shape[0]
    rows = _Rows(bm=m, stride=nb, bps=1, recompute=False)
    chunk = 2 * SUBLANES

    def hist(st, cols):
        h = _tm(st)
        return _pad_cols(h.reshape(1, h.shape[0] * nb, h.shape[2]), cols)

    hists = (hist(st_a, D_A), hist(st_q, D_QKVZ), hist(st_f, D_FF_PAD))

    def to_chunks(a2d):
        a = _tm(a2d.reshape(steps, nb, a2d.shape[1]))
        return jnp.pad(a, ((0, 0), (0, chunk - steps), (0, 0)))

    def delta_fn(qkvz, small):
        yb_b, s_fin = _delta(to_chunks(qkvz), to_chunks(small), w["alog"], w["dtb"], w["ng"],
                             st_d_all, layer, 4, chunk, steps, SUBLANES, "delta_sample")
        return _tm(yb_b[:, :steps]).reshape(m, D_B), s_fin

    def mixc_fn(xb):
        wc, bias = _sample_mix_coeffs(w, steps)
        return _mixc(xb, w["c"], w["layer"], w["lncg"], w["lncb"], wc, bias, m, steps, "mixc_sample")

    x2, x2_bf, tail_a, tail_q, s_fin, tail_f, vc = _layer(x, x_bf, p2d, w, rows, hists, delta_fn, mixc_fn)

    def from_tail(t, cols):
        return _tm(t.reshape(t.shape[1] // nb, nb, t.shape[2]))[:, :, :cols]

    new_v = _tm(vc[0].reshape(steps, nb, D_C))
    return (x2, x2_bf, from_tail(tail_a, D_A), from_tail(tail_q, 3 * D_B), s_fin,
            from_tail(tail_f, D_FF), new_v)


def kernel(x_prompt, x_sample, state_conv_a, state_conv_qkv, state_delta, state_conv_ffn,
           p_prompt, p_sample, ln_in_g, ln_in_b,
           w_in, conv_a_w, w_a_out, conv_b_w, a_log, dt_bias, norm_b_g, w_b_out,
           ln_c_g, ln_c_b, w_s, b_s, w_c_out, w_o, ln1_g, ln1_b,
           w_up, conv_f_w, w_down, w_pe, w_pg, ln2_g, ln2_b):
    nbp, seq, d = x_prompt.shape
    nbs, steps, _ = x_sample.shape
    mp, ms = nbp * seq, nbs * steps

    xp, xp_bf = _ln_in(x_prompt.reshape(mp, d), ln_in_g, ln_in_b, 256, "ln_in")
    xs, xs_bf = _ln_in(_tm(x_sample).reshape(ms, d), ln_in_g, ln_in_b, 256, "ln_in_s")

    outs_p = [[] for _ in range(4)]
    outs_s = [[] for _ in range(5)]
    big = _stacked_weights(w_in, w_a_out, w_b_out, w_c_out, w_o, w_up, w_down, w_pe, w_pg)
    for i in range(DEPTH):
        w = _layer_weights(i, big, conv_a_w, conv_b_w, a_log, dt_bias, norm_b_g, ln_c_g, ln_c_b,
                           w_s, b_s, ln1_g, ln1_b, conv_f_w, ln2_g, ln2_b)
        xp, xp_bf, *st_p = _layer_prompt(xp, xp_bf, p_prompt[i].reshape(mp, D_PLE), w, nbp, seq)
        for acc, val in zip(outs_p, st_p):
            acc.append(val)
        xs, xs_bf, *st_s = _layer_sample(xs, xs_bf, _tm(p_sample[i]).reshape(ms, D_PLE), w,
                                         state_conv_a[i], state_conv_qkv[i], state_delta, i,
                                         state_conv_ffn[i], nbs, steps)
        for acc, val in zip(outs_s, st_s):
            acc.append(val)

    y_prompt = xp.reshape(nbp, seq, d)
    y_sample = _tm(xs.reshape(steps, nbs, d))
    return (y_prompt, y_sample,
            jnp.stack(outs_p[0]), jnp.stack(outs_p[1]), jnp.stack(outs_p[2]), jnp.stack(outs_p[3]),
            jnp.stack(outs_s[0]), jnp.stack(outs_s[1]), jnp.stack(outs_s[2]), jnp.stack(outs_s[3]),
            jnp.stack(outs_s[4]))
```

```python
import functools
import math
from typing import NamedTuple

import jax
import jax.numpy as jnp
from jax import lax
from jax.experimental import pallas as pl
from jax.experimental.pallas import tpu as pltpu

F32 = jnp.float32
BF16 = jnp.bfloat16

D_MODEL = 2048
DEPTH = 2
D_PLE = 256
D_A = D_MODEL // 2
CONV_A = 3
H_B = 8
DK = 128
DV = 128
D_B = H_B * DV
CONV_B = 4
PROMPT_CHUNK = 128
D_C = D_MODEL // 2
CHUNK_C = 128
G_C = 8
D_FF = ((8 * D_MODEL // 3 + 127) // 128) * 128
CONV_F = 3
N_BRANCH = 3
DEEPNORM_ALPHA = (2 * DEPTH) ** 0.25
LN_EPS = 1e-5
RMS_EPS = 1e-6

LANES = 128
SUBLANES = 8
BF16_ROWS = 16
MXU_COLS = 256
D_FF_PAD = 5632
D_QKVZ = 4 * D_B
VMEM_LIMIT = 58 * 1024 * 1024

OFF_QKV = 3 * D_A
OFF_BETA = OFF_QKV + 4 * D_B
OFF_DEC = OFF_BETA + H_B
OFF_CU = OFF_DEC + H_B


class _Rows(NamedTuple):
    bm: int
    stride: int
    bps: int
    recompute: bool

    def halo(self, width):
        return BF16_ROWS if self.recompute else (width - 1) * self.stride

    def tail(self, width):
        return SUBLANES if self.recompute else (width - 1) * self.stride


def _params(*sem):
    return pltpu.CompilerParams(dimension_semantics=sem, vmem_limit_bytes=VMEM_LIMIT)


def _sigmoid(x):
    return 1.0 / (1.0 + jnp.exp(-x))


def _silu(x):
    return x * _sigmoid(x)


def _gelu(x):
    c = math.sqrt(2.0 / math.pi)
    return 0.5 * x * (1.0 + jnp.tanh(c * (x + 0.044715 * (x * x * x))))


def _softplus(x):
    return jnp.maximum(x, 0.0) + jnp.log1p(jnp.exp(-jnp.abs(x)))


def _ln(x, g, b):
    mu = jnp.mean(x, axis=-1, keepdims=True)
    xc = x - mu
    var = jnp.mean(xc * xc, axis=-1, keepdims=True)
    return xc * lax.rsqrt(var + LN_EPS) * g + b


def _l2n(t):
    return t * lax.rsqrt(jnp.sum(t * t, axis=-1, keepdims=True) + RMS_EPS)


def _dot(a, b):
    return jnp.dot(a.astype(BF16), b.astype(BF16), preferred_element_type=F32)


def _dot_nt(a, b):
    return lax.dot_general(a.astype(BF16), b.astype(BF16), (((1,), (1,)), ((), ())),
                           preferred_element_type=F32)


def _dot_tn(a, b):
    return lax.dot_general(a.astype(BF16), b.astype(BF16), (((0,), (0,)), ((), ())),
                           preferred_element_type=F32)


def _dot_hi(a, b):
    return jnp.dot(a, b, precision=lax.Precision.HIGHEST, preferred_element_type=F32)


def _mm_kernel(x_ref, w_ref, o_ref):
    o_ref[...] = _dot(x_ref[...], w_ref[...]).astype(o_ref.dtype)


def _mm(x, w, layer, bm, bn, out_dtype, name):
    m, k = x.shape
    n = w.shape[2]
    assert m % bm == 0 and n % bn == 0
    return pl.pallas_call(
        _mm_kernel,
        out_shape=jax.ShapeDtypeStruct((m, n), out_dtype),
        grid=(m // bm, n // bn),
        in_specs=[pl.BlockSpec((bm, k), lambda i, j: (i, 0)),
                  pl.BlockSpec((None, k, bn), lambda i, j, l=layer: (l, 0, j))],
        out_specs=pl.BlockSpec((bm, bn), lambda i, j: (i, j)),
        compiler_params=_params("parallel", "arbitrary"),
        name=name,
    )(x, w)


def _ln_in_kernel(x_ref, g_ref, b_ref, o_ref, obf_ref):
    r = _ln(x_ref[...], g_ref[...], b_ref[...])
    o_ref[...] = r
    obf_ref[...] = r.astype(BF16)


def _ln_in(x, g, b, bm, name):
    m, d = x.shape
    row = pl.BlockSpec((bm, d), lambda i: (i, 0))
    vec = pl.BlockSpec((1, d), lambda i: (0, 0))
    return pl.pallas_call(
        _ln_in_kernel,
        out_shape=(jax.ShapeDtypeStruct((m, d), F32), jax.ShapeDtypeStruct((m, d), BF16)),
        grid=(m // bm,),
        in_specs=[row, vec, vec],
        out_specs=(row, row),
        compiler_params=_params("parallel"),
        name=name,
    )(x, g.reshape(1, d), b.reshape(1, d))


def _stage_lhs(xh_ref, xm_ref, xcat_ref, halo):
    @pl.when(pl.program_id(1) == 0)
    def _():
        xcat_ref[0:halo, :] = xh_ref[...]
        xcat_ref[halo:, :] = xm_ref[...]
    return xcat_ref[...]


def _split_halo(res, hist_ref, rows, halo):
    is_start = (pl.program_id(0) % rows.bps) == 0
    return jnp.where(is_start, hist_ref[...], res[:halo]), res[halo:]


def _conv_time(top, main, cw, width, halo, stride):
    bm = main.shape[0]
    full = jnp.concatenate([top, main], axis=0)
    out = None
    for j in range(width):
        back = (width - 1 - j) * stride
        if back == 0:
            shifted = main
        elif back % SUBLANES == 0:
            shifted = full[halo - back:halo - back + bm, :]
        else:
            shifted = pltpu.roll(full, back, axis=0)[halo:, :]
        term = shifted * cw[j:j + 1, :]
        out = term if out is None else out + term
    return out


def _conv_call(kernel, rows, width, x_bf, weights, layer, hist, cw, n_cols, bn, out_dtype, name):
    m, k = x_bf.shape
    bm = rows.bm
    halo, tail = rows.halo(width), rows.tail(width)
    nblk = m // bm
    halo_blocks = bm // BF16_ROWS
    in_specs = [pl.BlockSpec((BF16_ROWS, k),
                             lambda i, j: (jnp.maximum(i * halo_blocks - 1, 0), 0)),
                pl.BlockSpec((bm, k), lambda i, j: (i, 0))]
    args = [x_bf, x_bf]
    for arr, off in weights:
        in_specs.append(pl.BlockSpec((None, k, bn),
                                     lambda i, j, o=off // bn, l=layer: (l, 0, o + j)))
        args.append(arr)
    in_specs += [pl.BlockSpec((None, halo, bn), lambda i, j: (i // rows.bps, 0, j)),
                 pl.BlockSpec((SUBLANES, bn), lambda i, j: (0, j))]
    args += [hist, cw]
    scratch = [pltpu.VMEM((halo + bm, k) if rows.recompute else (BF16_ROWS, LANES), BF16)]
    return pl.pallas_call(
        functools.partial(kernel, rows=rows, halo=halo, tail=tail),
        out_shape=(jax.ShapeDtypeStruct((m, n_cols), out_dtype),
                   jax.ShapeDtypeStruct((nblk, tail, n_cols), F32)),
        grid=(nblk, n_cols // bn),
        in_specs=in_specs,
        out_specs=(pl.BlockSpec((bm, bn), lambda i, j: (i, j)),
                   pl.BlockSpec((None, tail, bn), lambda i, j: (i, 0, j))),
        scratch_shapes=scratch,
        compiler_params=_params("parallel", "arbitrary"),
        name=name,
    )(*args)


def _mixa_kernel(xh_ref, xm_ref, wh_ref, wbg_ref, wcg_ref, hist_ref, cw_ref, y_ref, tail_ref,
                 xcat_ref, *, rows, halo, tail):
    bm = xm_ref.shape[0]
    if rows.recompute:
        lhs = _stage_lhs(xh_ref, xm_ref, xcat_ref, halo)
        u = _dot(lhs, wcg_ref[...]) * _dot(lhs, wh_ref[...])
        top, main = _split_halo(u, hist_ref, rows, halo)
    else:
        lhs = xm_ref[...]
        top, main = hist_ref[...], _dot(lhs, wcg_ref[...]) * _dot(lhs, wh_ref[...])
    conv = _conv_time(top, main, cw_ref[...], CONV_A, halo, rows.stride)
    y_ref[...] = (_dot(xm_ref[...], wbg_ref[...]) * conv).astype(y_ref.dtype)
    tail_ref[...] = main[bm - tail:, :]


def _qkvz_kernel(xh_ref, xm_ref, w_ref, hist_ref, cw_ref, o_ref, tail_ref, xcat_ref,
                 *, rows, halo, tail):
    bm, bn = o_ref.shape
    assert bn == D_B
    part = pl.program_id(1)
    if rows.recompute:
        lhs = _stage_lhs(xh_ref, xm_ref, xcat_ref, halo)

    @pl.when(part == 3)
    def _():
        main = _dot(xm_ref[...], w_ref[...])
        o_ref[...] = main
        tail_ref[...] = main[bm - tail:, :]

    def project(c):
        cs = slice(c * MXU_COLS, (c + 1) * MXU_COLS)
        if rows.recompute:
            return _split_halo(_dot(lhs, w_ref[:, cs]), hist_ref.at[:, cs], rows, halo)
        return hist_ref[:, cs], _dot(xm_ref[...], w_ref[:, cs])

    def conv_part(norm_scale):
        n_sub = bn // MXU_COLS
        nxt = project(0)
        for c in range(n_sub):
            cs = slice(c * MXU_COLS, (c + 1) * MXU_COLS)
            top, main = nxt
            if c + 1 < n_sub:
                nxt = project(c + 1)
            tail_ref[:, cs] = main[bm - tail:, :]
            s = _silu(_conv_time(top, main, cw_ref[:, cs], CONV_B, halo, rows.stride))
            for h in range(MXU_COLS // DK):
                sh = s[:, h * DK:(h + 1) * DK]
                if norm_scale is not None:
                    inv = lax.rsqrt(jnp.sum(sh * sh, axis=-1, keepdims=True) + RMS_EPS)
                    sh = sh * (inv * norm_scale) if norm_scale != 1.0 else sh * inv
                o_ref[:, c * MXU_COLS + h * DK:c * MXU_COLS + (h + 1) * DK] = sh

    for p, norm_scale in enumerate((DK ** -0.5, 1.0, None)):
        pl.when(part == p)(functools.partial(conv_part, norm_scale))


def _ffn_up_kernel(xh_ref, xm_ref, wg_ref, wu_ref, hist_ref, cw_ref, o_ref, tail_ref,
                   xcat_ref, *, rows, halo, tail):
    bm = xm_ref.shape[0]
    if rows.recompute:
        res = _dot(_stage_lhs(xh_ref, xm_ref, xcat_ref, halo), wg_ref[...])
        top, main = _split_halo(res, hist_ref, rows, halo)
    else:
        top, main = hist_ref[...], _dot(xm_ref[...], wg_ref[...])
    conv = _conv_time(top, main, cw_ref[...], CONV_F, halo, rows.stride)
    o_ref[...] = (_silu(conv) * _dot(xm_ref[...], wu_ref[...])).astype(o_ref.dtype)
    tail_ref[...] = main[bm - tail:, :]


def _neumann(xs, eye, order):
    ps = [eye - x for x in xs]
    xp = xs
    n = 2
    while n < order:
        xp = [_dot(x, x) for x in xp]
        ps = [p + _dot(p, x) for p, x in zip(ps, xp)]
        n *= 2
    return ps


def _unit_lower_inverse(lmats, chunk, blk, order):
    row = lax.broadcasted_iota(jnp.int32, (chunk, chunk), 0)
    col = lax.broadcasted_iota(jnp.int32, (chunk, chunk), 1)
    eye = (row == col).astype(F32)
    if order <= blk:
        return _neumann(lmats, eye, order)
    same = jnp.bitwise_xor(row, col) < blk
    d_inv = _neumann([jnp.where(same, l, 0.0) for l in lmats], eye, blk)
    ms = [_dot(d, jnp.where(same, 0.0, l)) for d, l in zip(d_inv, lmats)]
    m_inv = _neumann(ms, eye, order // blk)
    return [_dot(mi, d) for mi, d in zip(m_inv, d_inv)]


def _delta_kernel(*refs, nseq, chunk, valid, n_chunks, blk, order, fill_layers, has_prev):
    q_ref, k_ref, v_ref, z_ref, bg_ref, alog_ref, dtb_ref, ng_ref, s0_ref = refs[:9]
    y_ref, sfin_ref, s_scr = refs[9 + has_prev:]
    c = pl.program_id(1)

    @pl.when(c == 0)
    def _():
        s_scr[...] = s0_ref[...]

    row = lax.broadcasted_iota(jnp.int32, (chunk, chunk), 0)
    col = lax.broadcasted_iota(jnp.int32, (chunk, chunk), 1)
    causal = row >= col
    strict = row > col
    tri = causal.astype(F32)
    neg_a = -jnp.exp(alog_ref[...])
    dtb = dtb_ref[...]
    ng = ng_ref[...]

    betas, gcums, gcum_ts = [], [], []
    for s in range(nseq):
        beta = _sigmoid(bg_ref[s, :, 0:LANES])
        g = neg_a * _softplus(bg_ref[s, :, LANES:2 * LANES] + dtb)
        if valid < chunk:
            live = lax.broadcasted_iota(jnp.int32, (chunk, LANES), 0) < valid
            beta = jnp.where(live, beta, 0.0)
            g = jnp.where(live, g, 0.0)
        gcum = _dot_hi(tri, g)
        gpad = gcum
        if chunk < LANES:
            gpad = jnp.concatenate([gcum, jnp.zeros((LANES - chunk, LANES), F32)], axis=0)
        betas.append(beta)
        gcums.append(gcum)
        gcum_ts.append(gpad.T)

    probs = [(s, h) for s in range(nseq) for h in range(H_B)]
    cols = [slice(h * DK, (h + 1) * DK) for _, h in probs]
    qh = [q_ref[s, :, sl] for (s, _), sl in zip(probs, cols)]
    kh = [k_ref[s, :, sl] for (s, _), sl in zip(probs, cols)]
    vh = [v_ref[s, :, sl] for (s, _), sl in zip(probs, cols)]
    bcol = [betas[s][:, h:h + 1] for s, h in probs]
    gcol = [gcums[s][:, h:h + 1] for s, h in probs]
    glast = [gcums[s][chunk - 1:chunk, h:h + 1] for s, h in probs]
    decay = [jnp.exp(jnp.where(causal, gcums[s][:, h:h + 1] - gcum_ts[s][h:h + 1, 0:chunk], -jnp.inf))
             for s, h in probs]
    eg = [jnp.exp(g) for g in gcol]
    kb = [k * b for k, b in zip(kh, bcol)]

    kq = [_dot_nt(jnp.concatenate([a, q], axis=0), k) for a, q, k in zip(kb, qh, kh)]
    lmat = [jnp.where(strict, m[:chunk] * d, 0.0) for m, d in zip(kq, decay)]
    qk = [m[chunk:] * d for m, d in zip(kq, decay)]
    tmat = _unit_lower_inverse(lmat, chunk, blk, order)
    uw = [_dot(t, jnp.concatenate([v * b, a * e], axis=1))
          for t, v, b, a, e in zip(tmat, vh, bcol, kb, eg)]
    st = [s_scr[s, h] for s, h in probs]
    wq = [_dot(jnp.concatenate([m[:, DV:], q * e], axis=0), s)
          for m, q, e, s in zip(uw, qh, eg, st)]
    v_new = [m[:, :DV] - x[:chunk] for m, x in zip(uw, wq)]
    o = [x[chunk:] + _dot(a, vn) for x, a, vn in zip(wq, qk, v_new)]
    s_new = [s * jnp.exp(gl) + _dot_tn(k * jnp.exp(gl - g), vn)
             for s, gl, k, g, vn in zip(st, glast, kh, gcol, v_new)]
    for (s, h), sl, sn, oo in zip(probs, cols, s_new, o):
        s_scr[s, h] = sn
        on = oo * lax.rsqrt(jnp.mean(oo * oo, axis=-1, keepdims=True) + RMS_EPS) * ng
        y_ref[s, :, sl] = (on * _silu(z_ref[s, :, sl])).astype(y_ref.dtype)

    @pl.when(c == n_chunks - 1)
    def _():
        if fill_layers:
            for l in range(fill_layers):
                sfin_ref[l] = s_scr[...]
        else:
            sfin_ref[...] = s_scr[...]


def _delta(qkvz, bg, alog, dtb, ng, s0_all, layer, nseq, chunk, valid, blk, name, stacked=None):
    nb, seq, _ = qkvz.shape
    n_layers = s0_all.shape[0]
    n_chunks = seq // chunk
    order = chunk
    while order // 2 >= valid:
        order //= 2

    def part(p):
        return pl.BlockSpec((nseq, chunk, D_B), lambda b, c, o=p: (b, c, o))

    vec = pl.BlockSpec((1, LANES), lambda b, c: (0, 0))
    st = pl.BlockSpec((nseq, H_B, DK, DV), lambda b, c: (b, 0, 0, 0))
    st_in = pl.BlockSpec((None, nseq, H_B, DK, DV), lambda b, c, i=layer: (i, b, 0, 0, 0))
    in_specs = [part(0), part(1), part(2), part(3),
                pl.BlockSpec((nseq, chunk, 2 * LANES), lambda b, c: (b, c, 0)),
                vec, vec, vec, st_in]
    args = [qkvz, qkvz, qkvz, qkvz, bg, alog, dtb, ng, s0_all]
    st_shape = jax.ShapeDtypeStruct((n_layers, nb, H_B, DK, DV), F32)
    fill_layers, aliases = 0, {}
    if stacked is None:
        st_shape = jax.ShapeDtypeStruct((nb, H_B, DK, DV), F32)
    elif isinstance(stacked, str):
        fill_layers = n_layers
        st = pl.BlockSpec((n_layers, nseq, H_B, DK, DV), lambda b, c: (0, b, 0, 0, 0))
    else:
        st = st_in
        in_specs.append(pl.BlockSpec(memory_space=pl.ANY))
        args.append(stacked)
        aliases = {len(args) - 1: 1}
    return pl.pallas_call(
        functools.partial(_delta_kernel, nseq=nseq, chunk=chunk, valid=valid, n_chunks=n_chunks,
                          blk=blk, order=order, fill_layers=fill_layers, has_prev=bool(aliases)),
        out_shape=(jax.ShapeDtypeStruct((nb, seq, D_B), BF16), st_shape),
        grid=(nb // nseq, n_chunks),
        in_specs=in_specs,
        out_specs=(part(0), st),
        scratch_shapes=[pltpu.VMEM((nseq, H_B, DK, DV), F32)],
        input_output_aliases=aliases,
        compiler_params=_params("parallel", "arbitrary"),
        name=name,
    )(*args)


def _mixc_kernel(*refs, steps):
    if steps:
        x_ref, wcu_ref, wcv_ref, g_ref, b_ref, wc_ref, bias_ref, y_ref, vc_ref = refs
    else:
        x_ref, wcu_ref, wcv_ref, g_ref, b_ref, ws_ref, bias_ref, y_ref = refs
    x = x_ref[...]
    bm = x.shape[0]
    vc = _ln(_gelu(_dot(x, wcv_ref[...])), g_ref[...], b_ref[...])
    u = _gelu(_dot(x, wcu_ref[...]))
    if steps:
        nb = bm // steps
        vc_ref[...] = vc
        for t in range(steps):
            acc = None
            for s in range(t + 1):
                term = wc_ref[t * steps + s:t * steps + s + 1, :] * vc[s * nb:(s + 1) * nb, :]
                acc = term if acc is None else acc + term
            mixed = acc + bias_ref[t:t + 1, :]
            y_ref[t * nb:(t + 1) * nb, :] = (u[t * nb:(t + 1) * nb, :] * mixed).astype(y_ref.dtype)
    else:
        n_chunks = bm // CHUNK_C
        row = lax.broadcasted_iota(jnp.int32, (CHUNK_C, CHUNK_C), 0)
        col = lax.broadcasted_iota(jnp.int32, (CHUNK_C, CHUNK_C), 1)
        tri = row >= col
        cg = D_C // G_C
        for g in range(G_C):
            sl = slice(g * cg, (g + 1) * cg)
            wm = jnp.where(tri, ws_ref[g], 0.0)
            vcat = jnp.concatenate([vc[n * CHUNK_C:(n + 1) * CHUNK_C, sl] for n in range(n_chunks)],
                                   axis=1)
            mixed = _dot(wm, vcat)
            for n in range(n_chunks):
                rs = slice(n * CHUNK_C, (n + 1) * CHUNK_C)
                val = u[rs, sl] * (mixed[:, n * cg:(n + 1) * cg] + bias_ref[:, sl])
                y_ref[rs, sl] = val.astype(y_ref.dtype)


def _mixc(x_bf, w_c, layer, g, b, mix_w, bias, bm, steps, name):
    m, k = x_bf.shape
    full = lambda a: pl.BlockSpec(a.shape, lambda i: (0,) * a.ndim)
    row = pl.BlockSpec((bm, D_C), lambda i: (i, 0))
    out_shape = [jax.ShapeDtypeStruct((m, D_C), BF16)]
    out_specs = [row]
    if steps:
        out_shape.append(jax.ShapeDtypeStruct((m, D_C), F32))
        out_specs.append(row)
    return pl.pallas_call(
        functools.partial(_mixc_kernel, steps=steps),
        out_shape=tuple(out_shape),
        grid=(m // bm,),
        in_specs=[pl.BlockSpec((bm, k), lambda i: (i, 0)),
                  pl.BlockSpec((None, k, D_C), lambda i, l=layer: (l, 0, 0)),
                  pl.BlockSpec((None, k, D_C), lambda i, l=layer: (l, 0, 1)),
                  full(g), full(b), full(mix_w), full(bias)],
        out_specs=tuple(out_specs),
        compiler_params=_params("parallel"),
        name=name,
    )(x_bf, w_c, w_c, g, b, mix_w, bias)


def _merge_kernel(x_ref, ya_ref, yb_ref, yc_ref, g0_ref, g1_ref, g2_ref, wa_ref, wb_ref, wc_ref, o_ref):
    x = x_ref[...]
    m = _sigmoid(_dot(x, g0_ref[...])) * _dot(ya_ref[...], wa_ref[...])
    m = m + _sigmoid(_dot(x, g1_ref[...])) * _dot(yb_ref[...], wb_ref[...])
    m = m + _sigmoid(_dot(x, g2_ref[...])) * _dot(yc_ref[...], wc_ref[...])
    o_ref[...] = m.astype(o_ref.dtype)


def _merge(x_bf, ya, yb, yc, w_c, wa, wb, wc, layer, bm, bn=256):
    m = x_bf.shape[0]
    ys = pl.BlockSpec((bm, D_A), lambda i, j: (i, 0))
    ws = pl.BlockSpec((None, D_A, bn), lambda i, j, l=layer: (l, 0, j))

    def gate(branch):
        first = (2 * D_C + branch * D_MODEL) // bn
        return pl.BlockSpec((None, D_MODEL, bn), lambda i, j, o=first, l=layer: (l, 0, o + j))

    return pl.pallas_call(
        _merge_kernel,
        out_shape=jax.ShapeDtypeStruct((m, D_MODEL), BF16),
        grid=(m // bm, D_MODEL // bn),
        in_specs=[pl.BlockSpec((bm, D_MODEL), lambda i, j: (i, 0)), ys, ys, ys,
                  gate(0), gate(1), gate(2), ws, ws, ws],
        out_specs=pl.BlockSpec((bm, bn), lambda i, j: (i, j)),
        compiler_params=_params("parallel", "arbitrary"),
        name="merge",
    )(x_bf, ya, yb, yc, w_c, w_c, w_c, wa, wb, wc)


def _o_ln1_kernel(m_ref, w_ref, x_ref, g_ref, b_ref, o_ref, obf_ref):
    y = DEEPNORM_ALPHA * x_ref[...] + _dot(m_ref[...], w_ref[...])
    r = _ln(y, g_ref[...], b_ref[...])
    o_ref[...] = r
    obf_ref[...] = r.astype(BF16)


def _o_ln1(merged, w_o, layer, x, g, b, bm=256):
    m, d = x.shape
    row = pl.BlockSpec((bm, d), lambda i: (i, 0))
    vec = pl.BlockSpec((1, d), lambda i: (0, 0))
    return pl.pallas_call(
        _o_ln1_kernel,
        out_shape=(jax.ShapeDtypeStruct((m, d), F32), jax.ShapeDtypeStruct((m, d), BF16)),
        grid=(m // bm,),
        in_specs=[row, pl.BlockSpec((None, d, d), lambda i, l=layer: (l, 0, 0)), row, vec, vec],
        out_specs=(row, row),
        compiler_params=_params("parallel"),
        name="o_ln1",
    )(merged, w_o, x, g.reshape(1, d), b.reshape(1, d))


def _down_ln2_kernel(act_ref, wd_ref, xbf_ref, wpg_ref, p_ref, wpe_ref, xres_ref, g_ref, b_ref,
                     o_ref, obf_ref, acc_ref, *, nt):
    j = pl.program_id(1)
    bn = xres_ref.shape[1]
    ffn = _dot(act_ref[...], wd_ref[...])
    ple = _sigmoid(_dot(xbf_ref[...], wpg_ref[...])) * _dot(p_ref[...], wpe_ref[...])
    acc_ref[j] = DEEPNORM_ALPHA * xres_ref[...] + ffn + ple

    @pl.when(j == nt - 1)
    def _():
        tiles = [acc_ref[t] for t in range(nt)]
        inv_n = 1.0 / (nt * bn)
        mu = sum(jnp.sum(t, axis=-1, keepdims=True) for t in tiles) * inv_n
        cent = [t - mu for t in tiles]
        var = sum(jnp.sum(c * c, axis=-1, keepdims=True) for c in cent) * inv_n
        inv = lax.rsqrt(var + LN_EPS)
        for t, c in enumerate(cent):
            sl = slice(t * bn, (t + 1) * bn)
            r = c * inv * g_ref[:, sl] + b_ref[:, sl]
            o_ref[:, sl] = r
            obf_ref[:, sl] = r.astype(BF16)


def _down_ln2(act, w_down, x1, x1_bf, p, wpg, wpe, layer, g, b, bm, bn=512):
    m, d = x1.shape
    nt = d // bn
    kf = w_down.shape[1]
    row = pl.BlockSpec((bm, d), lambda i, j: (i, 0))
    vec = pl.BlockSpec((1, d), lambda i, j: (0, 0))
    return pl.pallas_call(
        functools.partial(_down_ln2_kernel, nt=nt),
        out_shape=(jax.ShapeDtypeStruct((m, d), F32), jax.ShapeDtypeStruct((m, d), BF16)),
        grid=(m // bm, nt),
        in_specs=[pl.BlockSpec((bm, kf), lambda i, j: (i, 0)),
                  pl.BlockSpec((None, kf, bn), lambda i, j, l=layer: (l, 0, j)),
                  row,
                  pl.BlockSpec((None, d, bn), lambda i, j, l=layer: (l, 0, j)),
                  pl.BlockSpec((bm, D_PLE), lambda i, j: (i, 0)),
                  pl.BlockSpec((None, D_PLE, bn), lambda i, j, l=layer: (l, 0, j)),
                  pl.BlockSpec((bm, bn), lambda i, j: (i, j)),
                  vec, vec],
        out_specs=(row, row),
        scratch_shapes=[pltpu.VMEM((nt, bm, bn), F32)],
        compiler_params=_params("parallel", "arbitrary"),
        name="down_ln2",
    )(act, w_down, x1_bf, wpg, p, wpe, x1, g.reshape(1, d), b.reshape(1, d))


def _pad_rows(w, rows):
    return jnp.pad(w, ((0, rows - w.shape[0]), (0, 0)))


def _pad_cols(w, cols):
    return jnp.pad(w, ((0, 0),) * (w.ndim - 1) + ((0, cols - w.shape[-1]),))


def _layer_weights(i, big, conv_a_w, conv_b_w, a_log, dt_bias, norm_b_g, ln_c_g, ln_c_b, w_s, b_s,
                   ln1_g, ln1_b, conv_f_w, ln2_g, ln2_b):
    w = dict(big)
    w["layer"] = i
    w["conv_a"] = _pad_rows(conv_a_w[i], SUBLANES)
    w["conv_b"] = _pad_cols(_pad_rows(conv_b_w[i], SUBLANES), D_QKVZ)
    w["conv_f"] = _pad_cols(_pad_rows(conv_f_w[i], SUBLANES), D_FF_PAD)
    w["alog"] = _pad_cols(a_log[i].reshape(1, H_B), LANES)
    w["dtb"] = _pad_cols(dt_bias[i].reshape(1, H_B), LANES)
    w["ng"] = norm_b_g[i].reshape(1, DV)
    w["lncg"] = ln_c_g[i].reshape(1, D_C)
    w["lncb"] = ln_c_b[i].reshape(1, D_C)
    w["ws"] = w_s[i]
    w["bias_c"] = jnp.repeat(b_s[i].T, D_C // G_C, axis=1)
    w["ln1g"], w["ln1b"] = ln1_g[i], ln1_b[i]
    w["ln2g"], w["ln2b"] = ln2_g[i], ln2_b[i]
    return w


def _stacked_weights(w_in, w_a_out, w_b_out, w_c_out, w_o, w_up, w_down, w_pe, w_pg):
    big = {}
    big["w_in"] = w_in
    big["c"] = w_in[:, :, OFF_CU:].astype(BF16)
    big["small"] = jnp.concatenate([_pad_cols(w_in[:, :, OFF_BETA:OFF_DEC], LANES),
                                    _pad_cols(w_in[:, :, OFF_DEC:OFF_CU], LANES)],
                                   axis=2).astype(BF16)
    big["a_out"], big["b_out"], big["c_out"] = w_a_out, w_b_out, w_c_out
    big["o"] = w_o.astype(BF16)
    big["w_up"] = w_up
    big["up_u"] = _pad_cols(w_up[:, :, D_FF:].astype(BF16), D_FF_PAD)
    big["down"] = w_down.astype(BF16)
    big["pe"] = w_pe.astype(BF16)
    big["pg"] = w_pg.astype(BF16)
    return big


def _sample_mix_coeffs(w, steps):
    cg = D_C // G_C
    wc = jnp.transpose(w["ws"][:, :steps, :steps], (1, 2, 0))
    wc = jnp.repeat(wc, cg, axis=2).reshape(steps * steps, D_C)
    return wc, w["bias_c"][:SUBLANES]


def _layer(x, x_bf, p2d, w, rows, hists, delta_fn, mixc_fn):
    hist_a, hist_q, hist_f = hists
    layer = w["layer"]
    ya, tail_a = _conv_call(_mixa_kernel, rows, CONV_A, x_bf,
                            [(w["w_in"], 0), (w["w_in"], D_A), (w["w_in"], 2 * D_A)], layer,
                            hist_a, w["conv_a"], D_A, 256, BF16, "mixa")
    qkvz, tail_q = _conv_call(_qkvz_kernel, rows, CONV_B, x_bf, [(w["w_in"], OFF_QKV)], layer,
                              hist_q, w["conv_b"], D_QKVZ, 1024, F32, "qkvz")
    small = _mm(x_bf, w["small"], layer, rows.bm, 2 * LANES, F32, "in_proj_small")
    yb, s_fin = delta_fn(qkvz, small)
    yc, *vc = mixc_fn(x_bf)

    merged = _merge(x_bf, ya, yb, yc, w["c"], w["a_out"], w["b_out"], w["c_out"], layer, rows.bm)
    x1, x1_bf = _o_ln1(merged, w["o"], layer, x, w["ln1g"], w["ln1b"])
    act, tail_f = _conv_call(_ffn_up_kernel, rows, CONV_F, x1_bf,
                             [(w["w_up"], 0), (w["up_u"], 0)], layer,
                             hist_f, w["conv_f"], D_FF_PAD, 512, BF16, "ffn_up")
    x2, x2_bf = _down_ln2(act, w["down"], x1, x1_bf, p2d, w["pg"], w["pe"], layer,
                          w["ln2g"], w["ln2b"], 512)
    return x2, x2_bf, tail_a, tail_q, s_fin, tail_f, vc


def _tm(a):
    return jnp.swapaxes(a, 0, 1)


def _layer_prompt(x, x_bf, p2d, w, nb, seq):
    rows = _Rows(bm=1024, stride=1, bps=seq // 1024, recompute=True)
    m = x.shape[0]
    hists = tuple(jnp.zeros((nb, BF16_ROWS, c), F32) for c in (D_A, D_QKVZ, D_FF_PAD))

    def delta_fn(qkvz, small):
        yb, s_fin = _delta(qkvz.reshape(nb, seq, D_QKVZ), small.reshape(nb, seq, 2 * LANES),
                           w["alog"], w["dtb"], w["ng"], jnp.zeros((1, nb, H_B, DK, DV), F32), 0,
                           1, PROMPT_CHUNK, PROMPT_CHUNK, SUBLANES, "delta_prompt")
        return yb.reshape(m, D_B), s_fin

    def mixc_fn(xb):
        return _mixc(xb, w["c"], w["layer"], w["lncg"], w["lncb"], w["ws"], w["bias_c"], 512, 0,
                     "mixc_prompt")

    x2, x2_bf, tail_a, tail_q, s_fin, tail_f, _ = _layer(x, x_bf, p2d, w, rows, hists, delta_fn, mixc_fn)
    last = slice(rows.bps - 1, None, rows.bps)
    new_a = tail_a[last, SUBLANES - (CONV_A - 1):, :]
    new_q = tail_q[last, SUBLANES - (CONV_B - 1):, :3 * D_B]
    new_f = tail_f[last, SUBLANES - (CONV_F - 1):, :D_FF]
    return x2, x2_bf, new_a, new_q, s_fin, new_f


def _layer_sample(x, x_bf, p2d, w, st_a, st_q, st_d_all, new_st_d, st_f, nb, steps):
    m = x.shape[0]
    layer = w["layer"]
    rows = _Rows(bm=m, stride=nb, bps=1, recompute=False)
    chunk = 2 * SUBLANES

    def hist(st, cols):
        h = _tm(st)
        return _pad_cols(h.reshape(1, h.shape[0] * nb, h.shape[2]), cols)

    hists = (hist(st_a, D_A), hist(st_q, D_QKVZ), hist(st_f, D_FF_PAD))

    def to_chunks(a2d):
        a = _tm(a2d.reshape(steps, nb, a2d.shape[1]))
        return jnp.pad(a, ((0, 0), (0, chunk - steps), (0, 0)))

    def delta_fn(qkvz, small):
        yb_b, s_fin = _delta(to_chunks(qkvz), to_chunks(small), w["alog"], w["dtb"], w["ng"],
                             st_d_all, layer, 4, chunk, steps, SUBLANES, "delta_sample",
                             stacked="first" if new_st_d is None else new_st_d)
        return _tm(yb_b[:, :steps]).reshape(m, D_B), s_fin

    def mixc_fn(xb):
        wc, bias = _sample_mix_coeffs(w, steps)
        return _mixc(xb, w["c"], w["layer"], w["lncg"], w["lncb"], wc, bias, m, steps, "mixc_sample")

    x2, x2_bf, tail_a, tail_q, s_fin, tail_f, vc = _layer(x, x_bf, p2d, w, rows, hists, delta_fn, mixc_fn)

    def from_tail(t, cols):
        return _tm(t.reshape(t.shape[1] // nb, nb, t.shape[2]))[:, :, :cols]

    new_v = _tm(vc[0].reshape(steps, nb, D_C))
    return (x2, x2_bf, from_tail(tail_a, D_A), from_tail(tail_q, 3 * D_B), s_fin,
            from_tail(tail_f, D_FF), new_v)


def kernel(x_prompt, x_sample, state_conv_a, state_conv_qkv, state_delta, state_conv_ffn,
           p_prompt, p_sample, ln_in_g, ln_in_b,
           w_in, conv_a_w, w_a_out, conv_b_w, a_log, dt_bias, norm_b_g, w_b_out,
           ln_c_g, ln_c_b, w_s, b_s, w_c_out, w_o, ln1_g, ln1_b,
           w_up, conv_f_w, w_down, w_pe, w_pg, ln2_g, ln2_b):
    nbp, seq, d = x_prompt.shape
    nbs, steps, _ = x_sample.shape
    mp, ms = nbp * seq, nbs * steps

    xp, xp_bf = _ln_in(x_prompt.reshape(mp, d), ln_in_g, ln_in_b, 256, "ln_in")
    xs, xs_bf = _ln_in(_tm(x_sample).reshape(ms, d), ln_in_g, ln_in_b, 256, "ln_in_s")

    outs_p = [[] for _ in range(4)]
    outs_s = [[] for _ in range(5)]
    new_st_d = None
    big = _stacked_weights(w_in, w_a_out, w_b_out, w_c_out, w_o, w_up, w_down, w_pe, w_pg)
    for i in range(DEPTH):
        w = _layer_weights(i, big, conv_a_w, conv_b_w, a_log, dt_bias, norm_b_g, ln_c_g, ln_c_b,
                           w_s, b_s, ln1_g, ln1_b, conv_f_w, ln2_g, ln2_b)
        xp, xp_bf, *st_p = _layer_prompt(xp, xp_bf, p_prompt[i].reshape(mp, D_PLE), w, nbp, seq)
        for acc, val in zip(outs_p, st_p):
            acc.append(val)
        xs, xs_bf, *st_s = _layer_sample(xs, xs_bf, _tm(p_sample[i]).reshape(ms, D_PLE), w,
                                         state_conv_a[i], state_conv_qkv[i], state_delta, new_st_d,
                                         state_conv_ffn[i], nbs, steps)
        new_st_d = st_s[2]
        for acc, val in zip(outs_s, st_s):
            acc.append(val)

    y_prompt = xp.reshape(nbp, seq, d)
    y_sample = _tm(xs.reshape(steps, nbs, d))
    return (y_prompt, y_sample,
            jnp.stack(outs_p[0]), jnp.stack(outs_p[1]), jnp.stack(outs_p[2]), jnp.stack(outs_p[3]),
            jnp.stack(outs_s[0]), jnp.stack(outs_s[1]), new_st_d, jnp.stack(outs_s[3]),
            jnp.stack(outs_s[4]))
```

```python
import functools
import math
from typing import NamedTuple

import jax
import jax.numpy as jnp
from jax import lax
from jax.experimental import pallas as pl
from jax.experimental.pallas import tpu as pltpu

F32 = jnp.float32
BF16 = jnp.bfloat16

D_MODEL = 2048
DEPTH = 2
D_PLE = 256
D_A = D_MODEL // 2
CONV_A = 3
H_B = 8
DK = 128
DV = 128
D_B = H_B * DV
CONV_B = 4
PROMPT_CHUNK = 128
D_C = D_MODEL // 2
CHUNK_C = 128
G_C = 8
D_FF = ((8 * D_MODEL // 3 + 127) // 128) * 128
CONV_F = 3
N_BRANCH = 3
DEEPNORM_ALPHA = (2 * DEPTH) ** 0.25
LN_EPS = 1e-5
RMS_EPS = 1e-6

LANES = 128
SUBLANES = 8
BF16_ROWS = 16
MXU_COLS = 256
D_FF_PAD = 5632
D_QKVZ = 4 * D_B
VMEM_LIMIT = 58 * 1024 * 1024

OFF_QKV = 3 * D_A
OFF_BETA = OFF_QKV + 4 * D_B
OFF_DEC = OFF_BETA + H_B
OFF_CU = OFF_DEC + H_B


class _Rows(NamedTuple):
    bm: int
    stride: int
    bps: int
    recompute: bool

    def halo(self, width):
        return BF16_ROWS if self.recompute else (width - 1) * self.stride

    def tail(self, width):
        return SUBLANES if self.recompute else (width - 1) * self.stride


def _params(*sem):
    return pltpu.CompilerParams(dimension_semantics=sem, vmem_limit_bytes=VMEM_LIMIT)


def _sigmoid(x):
    return 1.0 / (1.0 + jnp.exp(-x))


def _silu(x):
    return x * _sigmoid(x)


def _gelu(x):
    c = math.sqrt(2.0 / math.pi)
    return 0.5 * x * (1.0 + jnp.tanh(c * (x + 0.044715 * (x * x * x))))


def _softplus(x):
    return jnp.maximum(x, 0.0) + jnp.log1p(jnp.exp(-jnp.abs(x)))


def _ln(x, g, b):
    mu = jnp.mean(x, axis=-1, keepdims=True)
    xc = x - mu
    var = jnp.mean(xc * xc, axis=-1, keepdims=True)
    return xc * lax.rsqrt(var + LN_EPS) * g + b


def _l2n(t):
    return t * lax.rsqrt(jnp.sum(t * t, axis=-1, keepdims=True) + RMS_EPS)


def _dot(a, b):
    return jnp.dot(a.astype(BF16), b.astype(BF16), preferred_element_type=F32)


def _dot_nt(a, b):
    return lax.dot_general(a.astype(BF16), b.astype(BF16), (((1,), (1,)), ((), ())),
                           preferred_element_type=F32)


def _dot_tn(a, b):
    return lax.dot_general(a.astype(BF16), b.astype(BF16), (((0,), (0,)), ((), ())),
                           preferred_element_type=F32)


def _dot_hi(a, b):
    return jnp.dot(a, b, precision=lax.Precision.HIGHEST, preferred_element_type=F32)


def _mm_kernel(x_ref, w_ref, o_ref):
    o_ref[...] = _dot(x_ref[...], w_ref[...]).astype(o_ref.dtype)


def _mm(x, w, layer, bm, bn, out_dtype, name):
    m, k = x.shape
    n = w.shape[2]
    assert m % bm == 0 and n % bn == 0
    return pl.pallas_call(
        _mm_kernel,
        out_shape=jax.ShapeDtypeStruct((m, n), out_dtype),
        grid=(m // bm, n // bn),
        in_specs=[pl.BlockSpec((bm, k), lambda i, j: (i, 0)),
                  pl.BlockSpec((None, k, bn), lambda i, j, l=layer: (l, 0, j))],
        out_specs=pl.BlockSpec((bm, bn), lambda i, j: (i, j)),
        compiler_params=_params("parallel", "arbitrary"),
        name=name,
    )(x, w)


def _ln_in_kernel(x_ref, g_ref, b_ref, o_ref, obf_ref):
    r = _ln(x_ref[...], g_ref[...], b_ref[...])
    o_ref[...] = r
    obf_ref[...] = r.astype(BF16)


def _ln_in(x, g, b, bm, name):
    m, d = x.shape
    row = pl.BlockSpec((bm, d), lambda i: (i, 0))
    vec = pl.BlockSpec((1, d), lambda i: (0, 0))
    return pl.pallas_call(
        _ln_in_kernel,
        out_shape=(jax.ShapeDtypeStruct((m, d), F32), jax.ShapeDtypeStruct((m, d), BF16)),
        grid=(m // bm,),
        in_specs=[row, vec, vec],
        out_specs=(row, row),
        compiler_params=_params("parallel"),
        name=name,
    )(x, g.reshape(1, d), b.reshape(1, d))


def _stage_lhs(xh_ref, xm_ref, xcat_ref, halo):
    @pl.when(pl.program_id(1) == 0)
    def _():
        xcat_ref[0:halo, :] = xh_ref[...]
        xcat_ref[halo:, :] = xm_ref[...]
    return xcat_ref[...]


def _split_halo(res, hist_ref, rows, halo):
    is_start = (pl.program_id(0) % rows.bps) == 0
    return jnp.where(is_start, hist_ref[...], res[:halo]), res[halo:]


def _conv_time(top, main, cw, width, halo, stride):
    bm = main.shape[0]
    full = jnp.concatenate([top, main], axis=0)
    out = None
    for j in range(width):
        back = (width - 1 - j) * stride
        if back == 0:
            shifted = main
        elif back % SUBLANES == 0:
            shifted = full[halo - back:halo - back + bm, :]
        else:
            shifted = pltpu.roll(full, back, axis=0)[halo:, :]
        term = shifted * cw[j:j + 1, :]
        out = term if out is None else out + term
    return out


def _conv_call(kernel, rows, width, x_bf, weights, layer, hist, cw, n_cols, bn, out_dtype, name):
    m, k = x_bf.shape
    bm = rows.bm
    halo, tail = rows.halo(width), rows.tail(width)
    nblk = m // bm
    halo_blocks = bm // BF16_ROWS
    in_specs = [pl.BlockSpec((BF16_ROWS, k),
                             lambda i, j: (jnp.maximum(i * halo_blocks - 1, 0), 0)),
                pl.BlockSpec((bm, k), lambda i, j: (i, 0))]
    args = [x_bf, x_bf]
    for arr, off in weights:
        in_specs.append(pl.BlockSpec((None, k, bn),
                                     lambda i, j, o=off // bn, l=layer: (l, 0, o + j)))
        args.append(arr)
    in_specs += [pl.BlockSpec((None, halo, bn), lambda i, j: (i // rows.bps, 0, j)),
                 pl.BlockSpec((SUBLANES, bn), lambda i, j: (0, j))]
    args += [hist, cw]
    scratch = [pltpu.VMEM((halo + bm, k) if rows.recompute else (BF16_ROWS, LANES), BF16)]
    return pl.pallas_call(
        functools.partial(kernel, rows=rows, halo=halo, tail=tail),
        out_shape=(jax.ShapeDtypeStruct((m, n_cols), out_dtype),
                   jax.ShapeDtypeStruct((nblk, tail, n_cols), F32)),
        grid=(nblk, n_cols // bn),
        in_specs=in_specs,
        out_specs=(pl.BlockSpec((bm, bn), lambda i, j: (i, j)),
                   pl.BlockSpec((None, tail, bn), lambda i, j: (i, 0, j))),
        scratch_shapes=scratch,
        compiler_params=_params("parallel", "arbitrary"),
        name=name,
    )(*args)


def _mixa_kernel(xh_ref, xm_ref, wh_ref, wbg_ref, wcg_ref, hist_ref, cw_ref, y_ref, tail_ref,
                 xcat_ref, *, rows, halo, tail):
    bm = xm_ref.shape[0]
    if rows.recompute:
        lhs = _stage_lhs(xh_ref, xm_ref, xcat_ref, halo)
        u = _dot(lhs, wcg_ref[...]) * _dot(lhs, wh_ref[...])
        top, main = _split_halo(u, hist_ref, rows, halo)
    else:
        lhs = xm_ref[...]
        top, main = hist_ref[...], _dot(lhs, wcg_ref[...]) * _dot(lhs, wh_ref[...])
    conv = _conv_time(top, main, cw_ref[...], CONV_A, halo, rows.stride)
    y_ref[...] = (_dot(xm_ref[...], wbg_ref[...]) * conv).astype(y_ref.dtype)
    tail_ref[...] = main[bm - tail:, :]


def _qkvz_kernel(xh_ref, xm_ref, w_ref, hist_ref, cw_ref, o_ref, tail_ref, xcat_ref,
                 *, rows, halo, tail):
    bm, bn = o_ref.shape
    assert bn == D_B
    part = pl.program_id(1)
    if rows.recompute:
        lhs = _stage_lhs(xh_ref, xm_ref, xcat_ref, halo)

    @pl.when(part == 3)
    def _():
        main = _dot(xm_ref[...], w_ref[...])
        o_ref[...] = main
        tail_ref[...] = main[bm - tail:, :]

    def project(c):
        cs = slice(c * MXU_COLS, (c + 1) * MXU_COLS)
        if rows.recompute:
            return _split_halo(_dot(lhs, w_ref[:, cs]), hist_ref.at[:, cs], rows, halo)
        return hist_ref[:, cs], _dot(xm_ref[...], w_ref[:, cs])

    def conv_part(norm_scale):
        n_sub = bn // MXU_COLS
        nxt = project(0)
        for c in range(n_sub):
            cs = slice(c * MXU_COLS, (c + 1) * MXU_COLS)
            top, main = nxt
            if c + 1 < n_sub:
                nxt = project(c + 1)
            tail_ref[:, cs] = main[bm - tail:, :]
            s = _silu(_conv_time(top, main, cw_ref[:, cs], CONV_B, halo, rows.stride))
            for h in range(MXU_COLS // DK):
                sh = s[:, h * DK:(h + 1) * DK]
                if norm_scale is not None:
                    inv = lax.rsqrt(jnp.sum(sh * sh, axis=-1, keepdims=True) + RMS_EPS)
                    sh = sh * (inv * norm_scale) if norm_scale != 1.0 else sh * inv
                o_ref[:, c * MXU_COLS + h * DK:c * MXU_COLS + (h + 1) * DK] = sh

    for p, norm_scale in enumerate((DK ** -0.5, 1.0, None)):
        pl.when(part == p)(functools.partial(conv_part, norm_scale))


def _ffn_up_kernel(xh_ref, xm_ref, wg_ref, wu_ref, hist_ref, cw_ref, o_ref, tail_ref,
                   xcat_ref, *, rows, halo, tail):
    bm = xm_ref.shape[0]
    if rows.recompute:
        res = _dot(_stage_lhs(xh_ref, xm_ref, xcat_ref, halo), wg_ref[...])
        top, main = _split_halo(res, hist_ref, rows, halo)
    else:
        top, main = hist_ref[...], _dot(xm_ref[...], wg_ref[...])
    conv = _conv_time(top, main, cw_ref[...], CONV_F, halo, rows.stride)
    o_ref[...] = (_silu(conv) * _dot(xm_ref[...], wu_ref[...])).astype(o_ref.dtype)
    tail_ref[...] = main[bm - tail:, :]


def _neumann(xs, eye, order):
    ps = [eye - x for x in xs]
    xp = xs
    n = 2
    while n < order:
        xp = [_dot(x, x) for x in xp]
        ps = [p + _dot(p, x) for p, x in zip(ps, xp)]
        n *= 2
    return ps


def _unit_lower_inverse(lmats, chunk, blk, order):
    row = lax.broadcasted_iota(jnp.int32, (chunk, chunk), 0)
    col = lax.broadcasted_iota(jnp.int32, (chunk, chunk), 1)
    eye = (row == col).astype(F32)
    if order <= blk:
        return _neumann(lmats, eye, order)
    same = jnp.bitwise_xor(row, col) < blk
    d_inv = _neumann([jnp.where(same, l, 0.0) for l in lmats], eye, blk)
    ms = [_dot(d, jnp.where(same, 0.0, l)) for d, l in zip(d_inv, lmats)]
    m_inv = _neumann(ms, eye, order // blk)
    return [_dot(mi, d) for mi, d in zip(m_inv, d_inv)]


def _delta_kernel(*refs, nseq, chunk, valid, n_chunks, blk, order, fill_layers, has_prev):
    q_ref, k_ref, v_ref, z_ref, bg_ref, alog_ref, dtb_ref, ng_ref, s0_ref = refs[:9]
    y_ref, sfin_ref, s_scr = refs[9 + has_prev:]
    c = pl.program_id(1)

    @pl.when(c == 0)
    def _():
        s_scr[...] = s0_ref[...]

    row = lax.broadcasted_iota(jnp.int32, (chunk, chunk), 0)
    col = lax.broadcasted_iota(jnp.int32, (chunk, chunk), 1)
    causal = row >= col
    strict = row > col
    tri = causal.astype(F32)
    neg_a = -jnp.exp(alog_ref[...])
    dtb = dtb_ref[...]
    ng = ng_ref[...]

    betas, gcums, gcum_ts = [], [], []
    for s in range(nseq):
        beta = _sigmoid(bg_ref[s, :, 0:LANES])
        g = neg_a * _softplus(bg_ref[s, :, LANES:2 * LANES] + dtb)
        if valid < chunk:
            live = lax.broadcasted_iota(jnp.int32, (chunk, LANES), 0) < valid
            beta = jnp.where(live, beta, 0.0)
            g = jnp.where(live, g, 0.0)
        gcum = _dot_hi(tri, g)
        gpad = gcum
        if chunk < LANES:
            gpad = jnp.concatenate([gcum, jnp.zeros((LANES - chunk, LANES), F32)], axis=0)
        betas.append(beta)
        gcums.append(gcum)
        gcum_ts.append(gpad.T)

    probs = [(s, h) for s in range(nseq) for h in range(H_B)]
    cols = [slice(h * DK, (h + 1) * DK) for _, h in probs]
    qh = [q_ref[s, :, sl] for (s, _), sl in zip(probs, cols)]
    kh = [k_ref[s, :, sl] for (s, _), sl in zip(probs, cols)]
    vh = [v_ref[s, :, sl] for (s, _), sl in zip(probs, cols)]
    bcol = [betas[s][:, h:h + 1] for s, h in probs]
    gcol = [gcums[s][:, h:h + 1] for s, h in probs]
    glast = [gcums[s][chunk - 1:chunk, h:h + 1] for s, h in probs]
    decay = [jnp.exp(jnp.where(causal, gcums[s][:, h:h + 1] - gcum_ts[s][h:h + 1, 0:chunk], -jnp.inf))
             for s, h in probs]
    eg = [jnp.exp(g) for g in gcol]
    kb = [k * b for k, b in zip(kh, bcol)]

    kq = [_dot_nt(jnp.concatenate([a, q], axis=0), k) for a, q, k in zip(kb, qh, kh)]
    lmat = [jnp.where(strict, m[:chunk] * d, 0.0) for m, d in zip(kq, decay)]
    qk = [m[chunk:] * d for m, d in zip(kq, decay)]
    tmat = _unit_lower_inverse(lmat, chunk, blk, order)
    uw = [_dot(t, jnp.concatenate([v * b, a * e], axis=1))
          for t, v, b, a, e in zip(tmat, vh, bcol, kb, eg)]
    st = [s_scr[s, h] for s, h in probs]
    wq = [_dot(jnp.concatenate([m[:, DV:], q * e], axis=0), s)
          for m, q, e, s in zip(uw, qh, eg, st)]
    v_new = [m[:, :DV] - x[:chunk] for m, x in zip(uw, wq)]
    o = [x[chunk:] + _dot(a, vn) for x, a, vn in zip(wq, qk, v_new)]
    s_new = [s * jnp.exp(gl) + _dot_tn(k * jnp.exp(gl - g), vn)
             for s, gl, k, g, vn in zip(st, glast, kh, gcol, v_new)]
    for (s, h), sl, sn, oo in zip(probs, cols, s_new, o):
        s_scr[s, h] = sn
        on = oo * lax.rsqrt(jnp.mean(oo * oo, axis=-1, keepdims=True) + RMS_EPS) * ng
        y_ref[s, :, sl] = (on * _silu(z_ref[s, :, sl])).astype(y_ref.dtype)

    @pl.when(c == n_chunks - 1)
    def _():
        if fill_layers:
            for l in range(fill_layers):
                sfin_ref[l] = s_scr[...]
        else:
            sfin_ref[...] = s_scr[...]


def _delta(qkvz, bg, alog, dtb, ng, s0_all, layer, nseq, chunk, valid, blk, name, stacked=None):
    nb, seq, _ = qkvz.shape
    n_layers = s0_all.shape[0]
    n_chunks = seq // chunk
    order = chunk
    while order // 2 >= valid:
        order //= 2

    def part(p):
        return pl.BlockSpec((nseq, chunk, D_B), lambda b, c, o=p: (b, c, o))

    vec = pl.BlockSpec((1, LANES), lambda b, c: (0, 0))
    st = pl.BlockSpec((nseq, H_B, DK, DV), lambda b, c: (b, 0, 0, 0))
    st_in = pl.BlockSpec((None, nseq, H_B, DK, DV), lambda b, c, i=layer: (i, b, 0, 0, 0))
    in_specs = [part(0), part(1), part(2), part(3),
                pl.BlockSpec((nseq, chunk, 2 * LANES), lambda b, c: (b, c, 0)),
                vec, vec, vec, st_in]
    args = [qkvz, qkvz, qkvz, qkvz, bg, alog, dtb, ng, s0_all]
    st_shape = jax.ShapeDtypeStruct((n_layers, nb, H_B, DK, DV), F32)
    fill_layers, aliases = 0, {}
    if stacked is None:
        st_shape = jax.ShapeDtypeStruct((nb, H_B, DK, DV), F32)
    elif isinstance(stacked, str):
        fill_layers = n_layers
        st = pl.BlockSpec((n_layers, nseq, H_B, DK, DV), lambda b, c: (0, b, 0, 0, 0))
    else:
        st = st_in
        in_specs.append(pl.BlockSpec(memory_space=pl.ANY))
        args.append(stacked)
        aliases = {len(args) - 1: 1}
    return pl.pallas_call(
        functools.partial(_delta_kernel, nseq=nseq, chunk=chunk, valid=valid, n_chunks=n_chunks,
                          blk=blk, order=order, fill_layers=fill_layers, has_prev=bool(aliases)),
        out_shape=(jax.ShapeDtypeStruct((nb, seq, D_B), BF16), st_shape),
        grid=(nb // nseq, n_chunks),
        in_specs=in_specs,
        out_specs=(part(0), st),
        scratch_shapes=[pltpu.VMEM((nseq, H_B, DK, DV), F32)],
        input_output_aliases=aliases,
        compiler_params=_params("parallel", "arbitrary"),
        name=name,
    )(*args)


def _mixc_kernel(*refs, steps):
    if steps:
        x_ref, wcu_ref, wcv_ref, g_ref, b_ref, wc_ref, bias_ref, y_ref, vc_ref = refs
    else:
        x_ref, wcu_ref, wcv_ref, g_ref, b_ref, ws_ref, bias_ref, y_ref = refs
    x = x_ref[...]
    bm = x.shape[0]
    vc = _ln(_gelu(_dot(x, wcv_ref[...])), g_ref[...], b_ref[...])
    u = _gelu(_dot(x, wcu_ref[...]))
    if steps:
        nb = bm // steps
        vc_ref[...] = vc
        for t in range(steps):
            acc = None
            for s in range(t + 1):
                term = wc_ref[t * steps + s:t * steps + s + 1, :] * vc[s * nb:(s + 1) * nb, :]
                acc = term if acc is None else acc + term
            mixed = acc + bias_ref[t:t + 1, :]
            y_ref[t * nb:(t + 1) * nb, :] = (u[t * nb:(t + 1) * nb, :] * mixed).astype(y_ref.dtype)
    else:
        n_chunks = bm // CHUNK_C
        row = lax.broadcasted_iota(jnp.int32, (CHUNK_C, CHUNK_C), 0)
        col = lax.broadcasted_iota(jnp.int32, (CHUNK_C, CHUNK_C), 1)
        tri = row >= col
        cg = D_C // G_C
        for g in range(G_C):
            sl = slice(g * cg, (g + 1) * cg)
            wm = jnp.where(tri, ws_ref[g], 0.0)
            vcat = jnp.concatenate([vc[n * CHUNK_C:(n + 1) * CHUNK_C, sl] for n in range(n_chunks)],
                                   axis=1)
            mixed = _dot(wm, vcat)
            for n in range(n_chunks):
                rs = slice(n * CHUNK_C, (n + 1) * CHUNK_C)
                val = u[rs, sl] * (mixed[:, n * cg:(n + 1) * cg] + bias_ref[:, sl])
                y_ref[rs, sl] = val.astype(y_ref.dtype)


def _mixc(x_bf, w_c, layer, g, b, mix_w, bias, bm, steps, name):
    m, k = x_bf.shape
    full = lambda a: pl.BlockSpec(a.shape, lambda i: (0,) * a.ndim)
    row = pl.BlockSpec((bm, D_C), lambda i: (i, 0))
    out_shape = [jax.ShapeDtypeStruct((m, D_C), BF16)]
    out_specs = [row]
    if steps:
        out_shape.append(jax.ShapeDtypeStruct((m, D_C), F32))
        out_specs.append(row)
    return pl.pallas_call(
        functools.partial(_mixc_kernel, steps=steps),
        out_shape=tuple(out_shape),
        grid=(m // bm,),
        in_specs=[pl.BlockSpec((bm, k), lambda i: (i, 0)),
                  pl.BlockSpec((None, k, D_C), lambda i, l=layer: (l, 0, 0)),
                  pl.BlockSpec((None, k, D_C), lambda i, l=layer: (l, 0, 1)),
                  full(g), full(b), full(mix_w), full(bias)],
        out_specs=tuple(out_specs),
        compiler_params=_params("parallel"),
        name=name,
    )(x_bf, w_c, w_c, g, b, mix_w, bias)


def _merge_kernel(x_ref, ya_ref, yb_ref, yc_ref, g0_ref, g1_ref, g2_ref, wa_ref, wb_ref, wc_ref, o_ref):
    x = x_ref[...]
    m = _sigmoid(_dot(x, g0_ref[...])) * _dot(ya_ref[...], wa_ref[...])
    m = m + _sigmoid(_dot(x, g1_ref[...])) * _dot(yb_ref[...], wb_ref[...])
    m = m + _sigmoid(_dot(x, g2_ref[...])) * _dot(yc_ref[...], wc_ref[...])
    o_ref[...] = m.astype(o_ref.dtype)


def _merge(x_bf, ya, yb, yc, w_c, wa, wb, wc, layer, bm, bn=256):
    m = x_bf.shape[0]
    ys = pl.BlockSpec((bm, D_A), lambda i, j: (i, 0))
    ws = pl.BlockSpec((None, D_A, bn), lambda i, j, l=layer: (l, 0, j))

    def gate(branch):
        first = (2 * D_C + branch * D_MODEL) // bn
        return pl.BlockSpec((None, D_MODEL, bn), lambda i, j, o=first, l=layer: (l, 0, o + j))

    return pl.pallas_call(
        _merge_kernel,
        out_shape=jax.ShapeDtypeStruct((m, D_MODEL), BF16),
        grid=(m // bm, D_MODEL // bn),
        in_specs=[pl.BlockSpec((bm, D_MODEL), lambda i, j: (i, 0)), ys, ys, ys,
                  gate(0), gate(1), gate(2), ws, ws, ws],
        out_specs=pl.BlockSpec((bm, bn), lambda i, j: (i, j)),
        compiler_params=_params("parallel", "arbitrary"),
        name="merge",
    )(x_bf, ya, yb, yc, w_c, w_c, w_c, wa, wb, wc)


def _o_ln1_kernel(m_ref, w_ref, x_ref, g_ref, b_ref, o_ref, obf_ref):
    y = DEEPNORM_ALPHA * x_ref[...] + _dot(m_ref[...], w_ref[...])
    r = _ln(y, g_ref[...], b_ref[...])
    o_ref[...] = r
    obf_ref[...] = r.astype(BF16)


def _o_ln1(merged, w_o, layer, x, g, b, bm=256):
    m, d = x.shape
    row = pl.BlockSpec((bm, d), lambda i: (i, 0))
    vec = pl.BlockSpec((1, d), lambda i: (0, 0))
    return pl.pallas_call(
        _o_ln1_kernel,
        out_shape=(jax.ShapeDtypeStruct((m, d), F32), jax.ShapeDtypeStruct((m, d), BF16)),
        grid=(m // bm,),
        in_specs=[row, pl.BlockSpec((None, d, d), lambda i, l=layer: (l, 0, 0)), row, vec, vec],
        out_specs=(row, row),
        compiler_params=_params("parallel"),
        name="o_ln1",
    )(merged, w_o, x, g.reshape(1, d), b.reshape(1, d))


def _down_ln2_kernel(act_ref, wd_ref, xbf_ref, wpg_ref, p_ref, wpe_ref, xres_ref, g_ref, b_ref,
                     o_ref, obf_ref, acc_ref, *, nt):
    j = pl.program_id(1)
    bn = xres_ref.shape[1]
    ffn = _dot(act_ref[...], wd_ref[...])
    ple = _sigmoid(_dot(xbf_ref[...], wpg_ref[...])) * _dot(p_ref[...], wpe_ref[...])
    acc_ref[j] = DEEPNORM_ALPHA * xres_ref[...] + ffn + ple

    @pl.when(j == nt - 1)
    def _():
        tiles = [acc_ref[t] for t in range(nt)]
        inv_n = 1.0 / (nt * bn)
        mu = sum(jnp.sum(t, axis=-1, keepdims=True) for t in tiles) * inv_n
        cent = [t - mu for t in tiles]
        var = sum(jnp.sum(c * c, axis=-1, keepdims=True) for c in cent) * inv_n
        inv = lax.rsqrt(var + LN_EPS)
        for t, c in enumerate(cent):
            sl = slice(t * bn, (t + 1) * bn)
            r = c * inv * g_ref[:, sl] + b_ref[:, sl]
            o_ref[:, sl] = r
            obf_ref[:, sl] = r.astype(BF16)


def _down_ln2(act, w_down, x1, x1_bf, p, wpg, wpe, layer, g, b, bm, bn=512):
    m, d = x1.shape
    nt = d // bn
    kf = w_down.shape[1]
    row = pl.BlockSpec((bm, d), lambda i, j: (i, 0))
    vec = pl.BlockSpec((1, d), lambda i, j: (0, 0))
    return pl.pallas_call(
        functools.partial(_down_ln2_kernel, nt=nt),
        out_shape=(jax.ShapeDtypeStruct((m, d), F32), jax.ShapeDtypeStruct((m, d), BF16)),
        grid=(m // bm, nt),
        in_specs=[pl.BlockSpec((bm, kf), lambda i, j: (i, 0)),
                  pl.BlockSpec((None, kf, bn), lambda i, j, l=layer: (l, 0, j)),
                  row,
                  pl.BlockSpec((None, d, bn), lambda i, j, l=layer: (l, 0, j)),
                  pl.BlockSpec((bm, D_PLE), lambda i, j: (i, 0)),
                  pl.BlockSpec((None, D_PLE, bn), lambda i, j, l=layer: (l, 0, j)),
                  pl.BlockSpec((bm, bn), lambda i, j: (i, j)),
                  vec, vec],
        out_specs=(row, row),
        scratch_shapes=[pltpu.VMEM((nt, bm, bn), F32)],
        compiler_params=_params("parallel", "arbitrary"),
        name="down_ln2",
    )(act, w_down, x1_bf, wpg, p, wpe, x1, g.reshape(1, d), b.reshape(1, d))


def _shift_cast_kernel(a_ref, b_ref, last_ref, o_ref, *, shift, nj):
    bn = o_ref.shape[1]
    nxt = jnp.where(pl.program_id(2) == nj - 1, last_ref[...], b_ref[...])
    cat = jnp.concatenate([a_ref[...], nxt], axis=1)
    o_ref[...] = cat[:, shift:shift + bn].astype(BF16)


def _shift_cast(w, first, rb=512, bn=1024):
    layers, k, n = w.shape
    start = first // LANES * LANES
    shift = first - start
    n_out = n - first
    assert start % bn == 0 and n_out % bn == 0 and k % rb == 0 and 0 < shift < LANES
    nj = n_out // bn
    last = _pad_cols(w[:, :, n - shift:], LANES)
    lane_blocks = bn // LANES
    max_b = n // LANES - 1
    return pl.pallas_call(
        functools.partial(_shift_cast_kernel, shift=shift, nj=nj),
        out_shape=jax.ShapeDtypeStruct((layers, k, n_out), BF16),
        grid=(layers, k // rb, nj),
        in_specs=[pl.BlockSpec((None, rb, bn), lambda l, r, j, o=start // bn: (l, r, o + j)),
                  pl.BlockSpec((None, rb, LANES),
                               lambda l, r, j, o=start // LANES: (
                                   l, r, jnp.minimum(o + (j + 1) * lane_blocks, max_b))),
                  pl.BlockSpec((None, rb, LANES), lambda l, r, j: (l, r, 0))],
        out_specs=pl.BlockSpec((None, rb, bn), lambda l, r, j: (l, r, j)),
        compiler_params=_params("parallel", "parallel", "arbitrary"),
        name="shift_cast",
    )(w, w, last)


def _pad_cast_kernel(a_ref, o_ref):
    n = a_ref.shape[1]
    o_ref[:, :n] = a_ref[...].astype(BF16)
    o_ref[:, n:] = jnp.zeros((o_ref.shape[0], o_ref.shape[1] - n), BF16)


def _pad_cast_second_half(w, n_pad, rb=256):
    layers, k, n2 = w.shape
    n = n2 // 2
    assert n % LANES == 0 and n_pad % LANES == 0 and k % rb == 0
    return pl.pallas_call(
        _pad_cast_kernel,
        out_shape=jax.ShapeDtypeStruct((layers, k, n_pad), BF16),
        grid=(layers, k // rb),
        in_specs=[pl.BlockSpec((None, rb, n), lambda l, r: (l, r, 1))],
        out_specs=pl.BlockSpec((None, rb, n_pad), lambda l, r: (l, r, 0)),
        compiler_params=_params("parallel", "parallel"),
        name="pad_cast",
    )(w)


def _pad_rows(w, rows):
    return jnp.pad(w, ((0, rows - w.shape[0]), (0, 0)))


def _pad_cols(w, cols):
    return jnp.pad(w, ((0, 0),) * (w.ndim - 1) + ((0, cols - w.shape[-1]),))


def _layer_weights(i, big, conv_a_w, conv_b_w, a_log, dt_bias, norm_b_g, ln_c_g, ln_c_b, w_s, b_s,
                   ln1_g, ln1_b, conv_f_w, ln2_g, ln2_b):
    w = dict(big)
    w["layer"] = i
    w["conv_a"] = _pad_rows(conv_a_w[i], SUBLANES)
    w["conv_b"] = _pad_cols(_pad_rows(conv_b_w[i], SUBLANES), D_QKVZ)
    w["conv_f"] = _pad_cols(_pad_rows(conv_f_w[i], SUBLANES), D_FF_PAD)
    w["alog"] = _pad_cols(a_log[i].reshape(1, H_B), LANES)
    w["dtb"] = _pad_cols(dt_bias[i].reshape(1, H_B), LANES)
    w["ng"] = norm_b_g[i].reshape(1, DV)
    w["lncg"] = ln_c_g[i].reshape(1, D_C)
    w["lncb"] = ln_c_b[i].reshape(1, D_C)
    w["ws"] = w_s[i]
    w["bias_c"] = jnp.repeat(b_s[i].T, D_C // G_C, axis=1)
    w["ln1g"], w["ln1b"] = ln1_g[i], ln1_b[i]
    w["ln2g"], w["ln2b"] = ln2_g[i], ln2_b[i]
    return w


def _stacked_weights(w_in, w_a_out, w_b_out, w_c_out, w_o, w_up, w_down, w_pe, w_pg):
    big = {}
    big["w_in"] = w_in
    big["c"] = _shift_cast(w_in, OFF_CU)
    big["small"] = jnp.concatenate([_pad_cols(w_in[:, :, OFF_BETA:OFF_DEC], LANES),
                                    _pad_cols(w_in[:, :, OFF_DEC:OFF_CU], LANES)],
                                   axis=2).astype(BF16)
    big["a_out"], big["b_out"], big["c_out"] = w_a_out, w_b_out, w_c_out
    big["o"] = w_o.astype(BF16)
    big["w_up"] = w_up
    big["up_u"] = _pad_cast_second_half(w_up, D_FF_PAD)
    big["down"] = w_down.astype(BF16)
    big["pe"] = w_pe.astype(BF16)
    big["pg"] = w_pg.astype(BF16)
    return big


def _sample_mix_coeffs(w, steps):
    cg = D_C // G_C
    wc = jnp.transpose(w["ws"][:, :steps, :steps], (1, 2, 0))
    wc = jnp.repeat(wc, cg, axis=2).reshape(steps * steps, D_C)
    return wc, w["bias_c"][:SUBLANES]


def _layer(x, x_bf, p2d, w, rows, hists, delta_fn, mixc_fn):
    hist_a, hist_q, hist_f = hists
    layer = w["layer"]
    ya, tail_a = _conv_call(_mixa_kernel, rows, CONV_A, x_bf,
                            [(w["w_in"], 0), (w["w_in"], D_A), (w["w_in"], 2 * D_A)], layer,
                            hist_a, w["conv_a"], D_A, 256, BF16, "mixa")
    qkvz, tail_q = _conv_call(_qkvz_kernel, rows, CONV_B, x_bf, [(w["w_in"], OFF_QKV)], layer,
                              hist_q, w["conv_b"], D_QKVZ, 1024, F32, "qkvz")
    small = _mm(x_bf, w["small"], layer, rows.bm, 2 * LANES, F32, "in_proj_small")
    yb, s_fin = delta_fn(qkvz, small)
    yc, *vc = mixc_fn(x_bf)

    merged = _merge(x_bf, ya, yb, yc, w["c"], w["a_out"], w["b_out"], w["c_out"], layer, rows.bm)
    x1, x1_bf = _o_ln1(merged, w["o"], layer, x, w["ln1g"], w["ln1b"])
    act, tail_f = _conv_call(_ffn_up_kernel, rows, CONV_F, x1_bf,
                             [(w["w_up"], 0), (w["up_u"], 0)], layer,
                             hist_f, w["conv_f"], D_FF_PAD, 512, BF16, "ffn_up")
    x2, x2_bf = _down_ln2(act, w["down"], x1, x1_bf, p2d, w["pg"], w["pe"], layer,
                          w["ln2g"], w["ln2b"], 512)
    return x2, x2_bf, tail_a, tail_q, s_fin, tail_f, vc


def _tm(a):
    return jnp.swapaxes(a, 0, 1)


def _layer_prompt(x, x_bf, p2d, w, nb, seq):
    rows = _Rows(bm=1024, stride=1, bps=seq // 1024, recompute=True)
    m = x.shape[0]
    hists = tuple(jnp.zeros((nb, BF16_ROWS, c), F32) for c in (D_A, D_QKVZ, D_FF_PAD))

    def delta_fn(qkvz, small):
        yb, s_fin = _delta(qkvz.reshape(nb, seq, D_QKVZ), small.reshape(nb, seq, 2 * LANES),
                           w["alog"], w["dtb"], w["ng"], jnp.zeros((1, nb, H_B, DK, DV), F32), 0,
                           4, PROMPT_CHUNK, PROMPT_CHUNK, SUBLANES, "delta_prompt")
        return yb.reshape(m, D_B), s_fin

    def mixc_fn(xb):
        return _mixc(xb, w["c"], w["layer"], w["lncg"], w["lncb"], w["ws"], w["bias_c"], 512, 0,
                     "mixc_prompt")

    x2, x2_bf, tail_a, tail_q, s_fin, tail_f, _ = _layer(x, x_bf, p2d, w, rows, hists, delta_fn, mixc_fn)
    last = slice(rows.bps - 1, None, rows.bps)
    new_a = tail_a[last, SUBLANES - (CONV_A - 1):, :]
    new_q = tail_q[last, SUBLANES - (CONV_B - 1):, :3 * D_B]
    new_f = tail_f[last, SUBLANES - (CONV_F - 1):, :D_FF]
    return x2, x2_bf, new_a, new_q, s_fin, new_f


def _layer_sample(x, x_bf, p2d, w, st_a, st_q, st_d_all, new_st_d, st_f, nb, steps):
    m = x.shape[0]
    layer = w["layer"]
    rows = _Rows(bm=m, stride=nb, bps=1, recompute=False)
    chunk = 2 * SUBLANES

    def hist(st, cols):
        h = _tm(st)
        return _pad_cols(h.reshape(1, h.shape[0] * nb, h.shape[2]), cols)

    hists = (hist(st_a, D_A), hist(st_q, D_QKVZ), hist(st_f, D_FF_PAD))

    def to_chunks(a2d):
        a = _tm(a2d.reshape(steps, nb, a2d.shape[1]))
        return jnp.pad(a, ((0, 0), (0, chunk - steps), (0, 0)))

    def delta_fn(qkvz, small):
        yb_b, s_fin = _delta(to_chunks(qkvz), to_chunks(small), w["alog"], w["dtb"], w["ng"],
                             st_d_all, layer, 8, chunk, steps, SUBLANES, "delta_sample",
                             stacked="first" if new_st_d is None else new_st_d)
        return _tm(yb_b[:, :steps]).reshape(m, D_B), s_fin

    def mixc_fn(xb):
        wc, bias = _sample_mix_coeffs(w, steps)
        return _mixc(xb, w["c"], w["layer"], w["lncg"], w["lncb"], wc, bias, m, steps, "mixc_sample")

    x2, x2_bf, tail_a, tail_q, s_fin, tail_f, vc = _layer(x, x_bf, p2d, w, rows, hists, delta_fn, mixc_fn)

    def from_tail(t, cols):
        return _tm(t.reshape(t.shape[1] // nb, nb, t.shape[2]))[:, :, :cols]

    new_v = _tm(vc[0].reshape(steps, nb, D_C))
    return (x2, x2_bf, from_tail(tail_a, D_A), from_tail(tail_q, 3 * D_B), s_fin,
            from_tail(tail_f, D_FF), new_v)


def kernel(x_prompt, x_sample, state_conv_a, state_conv_qkv, state_delta, state_conv_ffn,
           p_prompt, p_sample, ln_in_g, ln_in_b,
           w_in, conv_a_w, w_a_out, conv_b_w, a_log, dt_bias, norm_b_g, w_b_out,
           ln_c_g, ln_c_b, w_s, b_s, w_c_out, w_o, ln1_g, ln1_b,
           w_up, conv_f_w, w_down, w_pe, w_pg, ln2_g, ln2_b):
    nbp, seq, d = x_prompt.shape
    nbs, steps, _ = x_sample.shape
    mp, ms = nbp * seq, nbs * steps

    xp, xp_bf = _ln_in(x_prompt.reshape(mp, d), ln_in_g, ln_in_b, 256, "ln_in")
    xs, xs_bf = _ln_in(_tm(x_sample).reshape(ms, d), ln_in_g, ln_in_b, 256, "ln_in_s")

    outs_p = [[] for _ in range(4)]
    outs_s = [[] for _ in range(5)]
    new_st_d = None
    big = _stacked_weights(w_in, w_a_out, w_b_out, w_c_out, w_o, w_up, w_down, w_pe, w_pg)
    for i in range(DEPTH):
        w = _layer_weights(i, big, conv_a_w, conv_b_w, a_log, dt_bias, norm_b_g, ln_c_g, ln_c_b,
                           w_s, b_s, ln1_g, ln1_b, conv_f_w, ln2_g, ln2_b)
        xp, xp_bf, *st_p = _layer_prompt(xp, xp_bf, p_prompt[i].reshape(mp, D_PLE), w, nbp, seq)
        for acc, val in zip(outs_p, st_p):
            acc.append(val)
        xs, xs_bf, *st_s = _layer_sample(xs, xs_bf, _tm(p_sample[i]).reshape(ms, D_PLE), w,
                                         state_conv_a[i], state_conv_qkv[i], state_delta, new_st_d,
                                         state_conv_ffn[i], nbs, steps)
        new_st_d = st_s[2]
        for acc, val in zip(outs_s, st_s):
            acc.append(val)

    y_prompt = xp.reshape(nbp, seq, d)
    y_sample = _tm(xs.reshape(steps, nbs, d))
    return (y_prompt, y_sample,
            jnp.stack(outs_p[0]), jnp.stack(outs_p[1]), jnp.stack(outs_p[2]), jnp.stack(outs_p[3]),
            jnp.stack(outs_s[0]), jnp.stack(outs_s[1]), new_st_d, jnp.stack(outs_s[3]),
            jnp.stack(outs_s[4]))
```

```python
import functools
import math
from typing import NamedTuple

import jax
import jax.numpy as jnp
from jax import lax
from jax.experimental import pallas as pl
from jax.experimental.pallas import tpu as pltpu

F32 = jnp.float32
BF16 = jnp.bfloat16

D_MODEL = 2048
DEPTH = 2
D_PLE = 256
D_A = D_MODEL // 2
CONV_A = 3
H_B = 8
DK = 128
DV = 128
D_B = H_B * DV
CONV_B = 4
PROMPT_CHUNK = 128
D_C = D_MODEL // 2
CHUNK_C = 128
G_C = 8
D_FF = ((8 * D_MODEL // 3 + 127) // 128) * 128
CONV_F = 3
N_BRANCH = 3
DEEPNORM_ALPHA = (2 * DEPTH) ** 0.25
LN_EPS = 1e-5
RMS_EPS = 1e-6

LANES = 128
SUBLANES = 8
BF16_ROWS = 16
MXU_COLS = 256
D_FF_PAD = 5632
D_QKVZ = 4 * D_B
VMEM_LIMIT = 58 * 1024 * 1024

OFF_QKV = 3 * D_A
OFF_BETA = OFF_QKV + 4 * D_B
OFF_DEC = OFF_BETA + H_B
OFF_CU = OFF_DEC + H_B
N_IN = OFF_CU + 2 * D_C + N_BRANCH * D_MODEL


class _Rows(NamedTuple):
    bm: int
    stride: int
    bps: int
    recompute: bool

    def halo(self, width):
        return BF16_ROWS if self.recompute else (width - 1) * self.stride

    def tail(self, width):
        return SUBLANES if self.recompute else (width - 1) * self.stride


def _params(*sem):
    return pltpu.CompilerParams(dimension_semantics=sem, vmem_limit_bytes=VMEM_LIMIT)


def _sigmoid(x):
    return 1.0 / (1.0 + jnp.exp(-x))


def _silu(x):
    return x * _sigmoid(x)


def _gelu(x):
    c = math.sqrt(2.0 / math.pi)
    return 0.5 * x * (1.0 + jnp.tanh(c * (x + 0.044715 * (x * x * x))))


def _softplus(x):
    return jnp.maximum(x, 0.0) + jnp.log1p(jnp.exp(-jnp.abs(x)))


def _ln(x, g, b):
    mu = jnp.mean(x, axis=-1, keepdims=True)
    xc = x - mu
    var = jnp.mean(xc * xc, axis=-1, keepdims=True)
    return xc * lax.rsqrt(var + LN_EPS) * g + b


def _dot(a, b):
    return jnp.dot(a.astype(BF16), b.astype(BF16), preferred_element_type=F32)


def _dot_nt(a, b):
    return lax.dot_general(a.astype(BF16), b.astype(BF16), (((1,), (1,)), ((), ())),
                           preferred_element_type=F32)


def _dot_tn(a, b):
    return lax.dot_general(a.astype(BF16), b.astype(BF16), (((0,), (0,)), ((), ())),
                           preferred_element_type=F32)


def _dot_w(a, w_ref, wt, sub=None):
    if wt:
        return _dot_nt(a, w_ref[...] if sub is None else w_ref[sub, :])
    return _dot(a, w_ref[...] if sub is None else w_ref[:, sub])


def _w_spec(k, bn, layer, first, wt):
    def col(g):
        return g[1] if len(g) > 1 else 0

    if wt:
        base = layer * N_IN + first
        assert base % SUBLANES == 0 and bn % SUBLANES == 0

        def rows(*g):
            r = base + col(g) * bn
            return (r if isinstance(r, int) else pl.multiple_of(r, SUBLANES)), 0

        return pl.BlockSpec((pl.Element(bn), pl.Element(k)), rows)
    assert first % bn == 0
    return pl.BlockSpec((None, k, bn), lambda *g, l=layer, o=first // bn: (l, 0, o + col(g)))


def _dot_hi(a, b):
    return jnp.dot(a, b, precision=lax.Precision.HIGHEST, preferred_element_type=F32)


def _mm_kernel(x_ref, w_ref, o_ref, *, wt):
    o_ref[...] = _dot_w(x_ref[...], w_ref, wt).astype(o_ref.dtype)


def _mm(x, w, layer, first_col, n, bm, bn, out_dtype, name, wt=False):
    m, k = x.shape
    assert m % bm == 0 and n % bn == 0
    return pl.pallas_call(
        functools.partial(_mm_kernel, wt=wt),
        out_shape=jax.ShapeDtypeStruct((m, n), out_dtype),
        grid=(m // bm, n // bn),
        in_specs=[pl.BlockSpec((bm, k), lambda i, j: (i, 0)),
                  _w_spec(k, bn, layer, first_col, wt)],
        out_specs=pl.BlockSpec((bm, bn), lambda i, j: (i, j)),
        compiler_params=_params("parallel", "arbitrary"),
        name=name,
    )(x, w)


def _ln_in_kernel(x_ref, g_ref, b_ref, o_ref, obf_ref):
    r = _ln(x_ref[...], g_ref[...], b_ref[...])
    o_ref[...] = r
    obf_ref[...] = r.astype(BF16)


def _ln_in(x, g, b, bm, name):
    m, d = x.shape
    row = pl.BlockSpec((bm, d), lambda i: (i, 0))
    vec = pl.BlockSpec((1, d), lambda i: (0, 0))
    return pl.pallas_call(
        _ln_in_kernel,
        out_shape=(jax.ShapeDtypeStruct((m, d), F32), jax.ShapeDtypeStruct((m, d), BF16)),
        grid=(m // bm,),
        in_specs=[row, vec, vec],
        out_specs=(row, row),
        compiler_params=_params("parallel"),
        name=name,
    )(x, g.reshape(1, d), b.reshape(1, d))


def _stage_lhs(xh_ref, xm_ref, xcat_ref, halo):
    @pl.when(pl.program_id(1) == 0)
    def _():
        xcat_ref[0:halo, :] = xh_ref[...]
        xcat_ref[halo:, :] = xm_ref[...]
    return xcat_ref[...]


def _split_halo(res, hist_ref, rows, halo):
    is_start = (pl.program_id(0) % rows.bps) == 0
    return jnp.where(is_start, hist_ref[...], res[:halo]), res[halo:]


def _conv_time(top, main, cw, width, halo, stride):
    bm = main.shape[0]
    full = jnp.concatenate([top, main], axis=0)
    out = None
    for j in range(width):
        back = (width - 1 - j) * stride
        if back == 0:
            shifted = main
        elif back % SUBLANES == 0:
            shifted = full[halo - back:halo - back + bm, :]
        else:
            shifted = pltpu.roll(full, back, axis=0)[halo:, :]
        term = shifted * cw[j:j + 1, :]
        out = term if out is None else out + term
    return out


def _conv_call(kernel, rows, width, x_bf, weights, layer, hist, cw, n_cols, bn, out_dtype, name,
               wt=False):
    m, k = x_bf.shape
    bm = rows.bm
    halo, tail = rows.halo(width), rows.tail(width)
    nblk = m // bm
    halo_blocks = bm // BF16_ROWS
    in_specs = [pl.BlockSpec((BF16_ROWS, k),
                             lambda i, j: (jnp.maximum(i * halo_blocks - 1, 0), 0)),
                pl.BlockSpec((bm, k), lambda i, j: (i, 0))]
    args = [x_bf, x_bf]
    for arr, off in weights:
        in_specs.append(_w_spec(k, bn, layer, off, wt))
        args.append(arr)
    in_specs += [pl.BlockSpec((None, halo, bn), lambda i, j: (i // rows.bps, 0, j)),
                 pl.BlockSpec((SUBLANES, bn), lambda i, j: (0, j))]
    args += [hist, cw]
    scratch = [pltpu.VMEM((halo + bm, k) if rows.recompute else (BF16_ROWS, LANES), BF16)]
    return pl.pallas_call(
        functools.partial(kernel, rows=rows, halo=halo, tail=tail, wt=wt),
        out_shape=(jax.ShapeDtypeStruct((m, n_cols), out_dtype),
                   jax.ShapeDtypeStruct((nblk, tail, n_cols), F32)),
        grid=(nblk, n_cols // bn),
        in_specs=in_specs,
        out_specs=(pl.BlockSpec((bm, bn), lambda i, j: (i, j)),
                   pl.BlockSpec((None, tail, bn), lambda i, j: (i, 0, j))),
        scratch_shapes=scratch,
        compiler_params=_params("parallel", "arbitrary"),
        name=name,
    )(*args)


def _mixa_kernel(xh_ref, xm_ref, wh_ref, wbg_ref, wcg_ref, hist_ref, cw_ref, y_ref, tail_ref,
                 xcat_ref, *, rows, halo, tail, wt):
    bm = xm_ref.shape[0]
    if rows.recompute:
        lhs = _stage_lhs(xh_ref, xm_ref, xcat_ref, halo)
        u = _dot_w(lhs, wcg_ref, wt) * _dot_w(lhs, wh_ref, wt)
        top, main = _split_halo(u, hist_ref, rows, halo)
    else:
        lhs = xm_ref[...]
        top, main = hist_ref[...], _dot_w(lhs, wcg_ref, wt) * _dot_w(lhs, wh_ref, wt)
    conv = _conv_time(top, main, cw_ref[...], CONV_A, halo, rows.stride)
    y_ref[...] = (_dot_w(xm_ref[...], wbg_ref, wt) * conv).astype(y_ref.dtype)
    tail_ref[...] = main[bm - tail:, :]


def _qkvz_kernel(xh_ref, xm_ref, w_ref, hist_ref, cw_ref, o_ref, tail_ref, xcat_ref,
                 *, rows, halo, tail, wt):
    bm, bn = o_ref.shape
    assert bn == D_B
    part = pl.program_id(1)
    if rows.recompute:
        lhs = _stage_lhs(xh_ref, xm_ref, xcat_ref, halo)

    @pl.when(part == 3)
    def _():
        main = _dot_w(xm_ref[...], w_ref, wt)
        o_ref[...] = main
        tail_ref[...] = main[bm - tail:, :]

    def project(c):
        cs = slice(c * MXU_COLS, (c + 1) * MXU_COLS)
        if rows.recompute:
            return _split_halo(_dot_w(lhs, w_ref, wt, cs), hist_ref.at[:, cs], rows, halo)
        return hist_ref[:, cs], _dot_w(xm_ref[...], w_ref, wt, cs)

    def conv_part(norm_scale):
        n_sub = bn // MXU_COLS
        nxt = project(0)
        for c in range(n_sub):
            cs = slice(c * MXU_COLS, (c + 1) * MXU_COLS)
            top, main = nxt
            if c + 1 < n_sub:
                nxt = project(c + 1)
            tail_ref[:, cs] = main[bm - tail:, :]
            s = _silu(_conv_time(top, main, cw_ref[:, cs], CONV_B, halo, rows.stride))
            for h in range(MXU_COLS // DK):
                sh = s[:, h * DK:(h + 1) * DK]
                if norm_scale is not None:
                    inv = lax.rsqrt(jnp.sum(sh * sh, axis=-1, keepdims=True) + RMS_EPS)
                    sh = sh * (inv * norm_scale) if norm_scale != 1.0 else sh * inv
                o_ref[:, c * MXU_COLS + h * DK:c * MXU_COLS + (h + 1) * DK] = sh

    for p, norm_scale in enumerate((DK ** -0.5, 1.0, None)):
        pl.when(part == p)(functools.partial(conv_part, norm_scale))


def _ffn_up_kernel(xh_ref, xm_ref, wg_ref, wu_ref, hist_ref, cw_ref, o_ref, tail_ref,
                   xcat_ref, *, rows, halo, tail, wt):
    bm = xm_ref.shape[0]
    if rows.recompute:
        res = _dot_w(_stage_lhs(xh_ref, xm_ref, xcat_ref, halo), wg_ref, wt)
        top, main = _split_halo(res, hist_ref, rows, halo)
    else:
        top, main = hist_ref[...], _dot_w(xm_ref[...], wg_ref, wt)
    conv = _conv_time(top, main, cw_ref[...], CONV_F, halo, rows.stride)
    o_ref[...] = (_silu(conv) * _dot_w(xm_ref[...], wu_ref, wt)).astype(o_ref.dtype)
    tail_ref[...] = main[bm - tail:, :]


def _neumann(xs, eye, order):
    ps = [eye - x for x in xs]
    xp = xs
    n = 2
    while n < order:
        xp = [_dot(x, x) for x in xp]
        ps = [p + _dot(p, x) for p, x in zip(ps, xp)]
        n *= 2
    return ps


def _unit_lower_inverse(lmats, chunk, blk, order):
    row = lax.broadcasted_iota(jnp.int32, (chunk, chunk), 0)
    col = lax.broadcasted_iota(jnp.int32, (chunk, chunk), 1)
    eye = (row == col).astype(F32)
    if order <= blk:
        return _neumann(lmats, eye, order)
    same = jnp.bitwise_xor(row, col) < blk
    d_inv = _neumann([jnp.where(same, l, 0.0) for l in lmats], eye, blk)
    ms = [_dot(d, jnp.where(same, 0.0, l)) for d, l in zip(d_inv, lmats)]
    m_inv = _neumann(ms, eye, order // blk)
    return [_dot(mi, d) for mi, d in zip(m_inv, d_inv)]


def _delta_kernel(*refs, nseq, chunk, valid, n_chunks, blk, order, fill_layers, has_prev):
    q_ref, k_ref, v_ref, z_ref, bg_ref, alog_ref, dtb_ref, ng_ref, s0_ref = refs[:9]
    y_ref, sfin_ref, s_scr = refs[9 + has_prev:]
    c = pl.program_id(1)

    @pl.when(c == 0)
    def _():
        s_scr[...] = s0_ref[...]

    row = lax.broadcasted_iota(jnp.int32, (chunk, chunk), 0)
    col = lax.broadcasted_iota(jnp.int32, (chunk, chunk), 1)
    causal = row >= col
    strict = row > col
    tri = causal.astype(F32)
    neg_a = -jnp.exp(alog_ref[...])
    dtb = dtb_ref[...]
    ng = ng_ref[...]

    betas, gcums, gcum_ts = [], [], []
    for s in range(nseq):
        raw = bg_ref[s]
        beta = _sigmoid(raw)
        g = neg_a * _softplus(raw + dtb)
        if valid < chunk:
            live = lax.broadcasted_iota(jnp.int32, (chunk, LANES), 0) < valid
            beta = jnp.where(live, beta, 0.0)
            g = jnp.where(live, g, 0.0)
        gcum = _dot_hi(tri, g)
        gpad = gcum
        if chunk < LANES:
            gpad = jnp.concatenate([gcum, jnp.zeros((LANES - chunk, LANES), F32)], axis=0)
        betas.append(beta)
        gcums.append(gcum)
        gcum_ts.append(gpad.T)

    probs = [(s, h) for s in range(nseq) for h in range(H_B)]
    cols = [slice(h * DK, (h + 1) * DK) for _, h in probs]
    qh = [q_ref[s, :, sl] for (s, _), sl in zip(probs, cols)]
    kh = [k_ref[s, :, sl] for (s, _), sl in zip(probs, cols)]
    vh = [v_ref[s, :, sl] for (s, _), sl in zip(probs, cols)]
    bcol = [betas[s][:, h:h + 1] for s, h in probs]
    gcol = [gcums[s][:, H_B + h:H_B + h + 1] for s, h in probs]
    glast = [g[chunk - 1:chunk, :] for g in gcol]
    decay = [jnp.exp(jnp.where(causal, g - gcum_ts[s][H_B + h:H_B + h + 1, 0:chunk], -jnp.inf))
             for g, (s, h) in zip(gcol, probs)]
    eg = [jnp.exp(g) for g in gcol]
    kb = [k * b for k, b in zip(kh, bcol)]

    kq = [_dot_nt(jnp.concatenate([a, q], axis=0), k) for a, q, k in zip(kb, qh, kh)]
    lmat = [jnp.where(strict, m[:chunk] * d, 0.0) for m, d in zip(kq, decay)]
    qk = [m[chunk:] * d for m, d in zip(kq, decay)]
    tmat = _unit_lower_inverse(lmat, chunk, blk, order)
    uw = [_dot(t, jnp.concatenate([v * b, a * e], axis=1))
          for t, v, b, a, e in zip(tmat, vh, bcol, kb, eg)]
    st = [s_scr[s, h] for s, h in probs]
    wq = [_dot(jnp.concatenate([m[:, DV:], q * e], axis=0), s)
          for m, q, e, s in zip(uw, qh, eg, st)]
    v_new = [m[:, :DV] - x[:chunk] for m, x in zip(uw, wq)]
    o = [x[chunk:] + _dot(a, vn) for x, a, vn in zip(wq, qk, v_new)]
    s_new = [s * jnp.exp(gl) + _dot_tn(k * jnp.exp(gl - g), vn)
             for s, gl, k, g, vn in zip(st, glast, kh, gcol, v_new)]
    for (s, h), sl, sn, oo in zip(probs, cols, s_new, o):
        s_scr[s, h] = sn
        on = oo * lax.rsqrt(jnp.mean(oo * oo, axis=-1, keepdims=True) + RMS_EPS) * ng
        y_ref[s, :, sl] = (on * _silu(z_ref[s, :, sl])).astype(y_ref.dtype)

    @pl.when(c == n_chunks - 1)
    def _():
        if fill_layers:
            for l in range(fill_layers):
                sfin_ref[l] = s_scr[...]
        else:
            sfin_ref[...] = s_scr[...]


def _delta(qkvz, bg, alog, dtb, ng, s0_all, layer, nseq, chunk, valid, blk, name, stacked=None):
    nb, seq, _ = qkvz.shape
    n_layers = s0_all.shape[0]
    n_chunks = seq // chunk
    order = chunk
    while order // 2 >= valid:
        order //= 2

    def part(p):
        return pl.BlockSpec((nseq, chunk, D_B), lambda b, c, o=p: (b, c, o))

    vec = pl.BlockSpec((1, LANES), lambda b, c: (0, 0))
    st = pl.BlockSpec((nseq, H_B, DK, DV), lambda b, c: (b, 0, 0, 0))
    st_in = pl.BlockSpec((None, nseq, H_B, DK, DV), lambda b, c, i=layer: (i, b, 0, 0, 0))
    in_specs = [part(0), part(1), part(2), part(3),
                pl.BlockSpec((nseq, chunk, LANES), lambda b, c: (b, c, 0)),
                vec, vec, vec, st_in]
    args = [qkvz, qkvz, qkvz, qkvz, bg, alog, dtb, ng, s0_all]
    st_shape = jax.ShapeDtypeStruct((n_layers, nb, H_B, DK, DV), F32)
    fill_layers, aliases = 0, {}
    if stacked is None:
        st_shape = jax.ShapeDtypeStruct((nb, H_B, DK, DV), F32)
    elif isinstance(stacked, str):
        fill_layers = n_layers
        st = pl.BlockSpec((n_layers, nseq, H_B, DK, DV), lambda b, c: (0, b, 0, 0, 0))
    else:
        st = st_in
        in_specs.append(pl.BlockSpec(memory_space=pl.ANY))
        args.append(stacked)
        aliases = {len(args) - 1: 1}
    return pl.pallas_call(
        functools.partial(_delta_kernel, nseq=nseq, chunk=chunk, valid=valid, n_chunks=n_chunks,
                          blk=blk, order=order, fill_layers=fill_layers, has_prev=bool(aliases)),
        out_shape=(jax.ShapeDtypeStruct((nb, seq, D_B), BF16), st_shape),
        grid=(nb // nseq, n_chunks),
        in_specs=in_specs,
        out_specs=(part(0), st),
        scratch_shapes=[pltpu.VMEM((nseq, H_B, DK, DV), F32)],
        input_output_aliases=aliases,
        compiler_params=_params("parallel", "arbitrary"),
        name=name,
    )(*args)


def _mixc_kernel(*refs, steps):
    if steps:
        x_ref, wcu_ref, wcv_ref, g_ref, b_ref, wc_ref, bias_ref, y_ref, vc_ref = refs
    else:
        x_ref, wcu_ref, wcv_ref, g_ref, b_ref, ws_ref, bias_ref, y_ref = refs
    x = x_ref[...]
    bm = x.shape[0]
    vc = _ln(_gelu(_dot_w(x, wcv_ref, True)), g_ref[...], b_ref[...])
    u = _gelu(_dot_w(x, wcu_ref, True))
    if steps:
        nb = bm // steps
        vc_ref[...] = vc
        for t in range(steps):
            acc = None
            for s in range(t + 1):
                term = wc_ref[t * steps + s:t * steps + s + 1, :] * vc[s * nb:(s + 1) * nb, :]
                acc = term if acc is None else acc + term
            mixed = acc + bias_ref[t:t + 1, :]
            y_ref[t * nb:(t + 1) * nb, :] = (u[t * nb:(t + 1) * nb, :] * mixed).astype(y_ref.dtype)
    else:
        n_chunks = bm // CHUNK_C
        row = lax.broadcasted_iota(jnp.int32, (CHUNK_C, CHUNK_C), 0)
        col = lax.broadcasted_iota(jnp.int32, (CHUNK_C, CHUNK_C), 1)
        tri = row >= col
        cg = D_C // G_C
        for g in range(G_C):
            sl = slice(g * cg, (g + 1) * cg)
            wm = jnp.where(tri, ws_ref[g], 0.0)
            vcat = jnp.concatenate([vc[n * CHUNK_C:(n + 1) * CHUNK_C, sl] for n in range(n_chunks)],
                                   axis=1)
            mixed = _dot(wm, vcat)
            for n in range(n_chunks):
                rs = slice(n * CHUNK_C, (n + 1) * CHUNK_C)
                val = u[rs, sl] * (mixed[:, n * cg:(n + 1) * cg] + bias_ref[:, sl])
                y_ref[rs, sl] = val.astype(y_ref.dtype)


def _mixc(x_bf, w_in_t, layer, g, b, mix_w, bias, bm, steps, name):
    m, k = x_bf.shape
    full = lambda a: pl.BlockSpec(a.shape, lambda i: (0,) * a.ndim)
    row = pl.BlockSpec((bm, D_C), lambda i: (i, 0))
    out_shape = [jax.ShapeDtypeStruct((m, D_C), BF16)]
    out_specs = [row]
    if steps:
        out_shape.append(jax.ShapeDtypeStruct((m, D_C), F32))
        out_specs.append(row)
    return pl.pallas_call(
        functools.partial(_mixc_kernel, steps=steps),
        out_shape=tuple(out_shape),
        grid=(m // bm,),
        in_specs=[pl.BlockSpec((bm, k), lambda i: (i, 0)),
                  _w_spec(k, D_C, layer, OFF_CU, True), _w_spec(k, D_C, layer, OFF_CU + D_C, True),
                  full(g), full(b), full(mix_w), full(bias)],
        out_specs=tuple(out_specs),
        compiler_params=_params("parallel"),
        name=name,
    )(x_bf, w_in_t, w_in_t, g, b, mix_w, bias)


def _merge_kernel(x_ref, ya_ref, yb_ref, yc_ref, g0_ref, g1_ref, g2_ref, wa_ref, wb_ref, wc_ref, o_ref):
    x = x_ref[...]
    m = _sigmoid(_dot_w(x, g0_ref, True)) * _dot(ya_ref[...], wa_ref[...])
    m = m + _sigmoid(_dot_w(x, g1_ref, True)) * _dot(yb_ref[...], wb_ref[...])
    m = m + _sigmoid(_dot_w(x, g2_ref, True)) * _dot(yc_ref[...], wc_ref[...])
    o_ref[...] = m.astype(o_ref.dtype)


def _merge(x_bf, ya, yb, yc, w_in_t, wa, wb, wc, layer, bm, bn=256):
    m = x_bf.shape[0]
    ys = pl.BlockSpec((bm, D_A), lambda i, j: (i, 0))
    ws = pl.BlockSpec((None, D_A, bn), lambda i, j, l=layer: (l, 0, j))

    def gate(branch):
        return _w_spec(D_MODEL, bn, layer, OFF_CU + 2 * D_C + branch * D_MODEL, True)

    return pl.pallas_call(
        _merge_kernel,
        out_shape=jax.ShapeDtypeStruct((m, D_MODEL), BF16),
        grid=(m // bm, D_MODEL // bn),
        in_specs=[pl.BlockSpec((bm, D_MODEL), lambda i, j: (i, 0)), ys, ys, ys,
                  gate(0), gate(1), gate(2), ws, ws, ws],
        out_specs=pl.BlockSpec((bm, bn), lambda i, j: (i, j)),
        compiler_params=_params("parallel", "arbitrary"),
        name="merge",
    )(x_bf, ya, yb, yc, w_in_t, w_in_t, w_in_t, wa, wb, wc)


def _o_ln1_kernel(m_ref, w_ref, x_ref, g_ref, b_ref, o_ref, obf_ref):
    y = DEEPNORM_ALPHA * x_ref[...] + _dot(m_ref[...], w_ref[...])
    r = _ln(y, g_ref[...], b_ref[...])
    o_ref[...] = r
    obf_ref[...] = r.astype(BF16)


def _o_ln1(merged, w_o, layer, x, g, b, bm=256):
    m, d = x.shape
    row = pl.BlockSpec((bm, d), lambda i: (i, 0))
    vec = pl.BlockSpec((1, d), lambda i: (0, 0))
    return pl.pallas_call(
        _o_ln1_kernel,
        out_shape=(jax.ShapeDtypeStruct((m, d), F32), jax.ShapeDtypeStruct((m, d), BF16)),
        grid=(m // bm,),
        in_specs=[row, pl.BlockSpec((None, d, d), lambda i, l=layer: (l, 0, 0)), row, vec, vec],
        out_specs=(row, row),
        compiler_params=_params("parallel"),
        name="o_ln1",
    )(merged, w_o, x, g.reshape(1, d), b.reshape(1, d))


def _down_ln2_kernel(act_ref, wd_ref, xbf_ref, wpg_ref, p_ref, wpe_ref, xres_ref, g_ref, b_ref,
                     o_ref, obf_ref, acc_ref, *, nt):
    j = pl.program_id(1)
    bn = xres_ref.shape[1]
    ffn = _dot(act_ref[...], wd_ref[...])
    ple = _sigmoid(_dot(xbf_ref[...], wpg_ref[...])) * _dot(p_ref[...], wpe_ref[...])
    acc_ref[j] = DEEPNORM_ALPHA * xres_ref[...] + ffn + ple

    @pl.when(j == nt - 1)
    def _():
        tiles = [acc_ref[t] for t in range(nt)]
        inv_n = 1.0 / (nt * bn)
        mu = sum(jnp.sum(t, axis=-1, keepdims=True) for t in tiles) * inv_n
        cent = [t - mu for t in tiles]
        var = sum(jnp.sum(c * c, axis=-1, keepdims=True) for c in cent) * inv_n
        inv = lax.rsqrt(var + LN_EPS)
        for t, c in enumerate(cent):
            sl = slice(t * bn, (t + 1) * bn)
            r = c * inv * g_ref[:, sl] + b_ref[:, sl]
            o_ref[:, sl] = r
            obf_ref[:, sl] = r.astype(BF16)


def _down_ln2(act, w_down, x1, x1_bf, p, wpg, wpe, layer, g, b, bm, bn=512):
    m, d = x1.shape
    nt = d // bn
    kf = w_down.shape[1]
    row = pl.BlockSpec((bm, d), lambda i, j: (i, 0))
    vec = pl.BlockSpec((1, d), lambda i, j: (0, 0))
    return pl.pallas_call(
        functools.partial(_down_ln2_kernel, nt=nt),
        out_shape=(jax.ShapeDtypeStruct((m, d), F32), jax.ShapeDtypeStruct((m, d), BF16)),
        grid=(m // bm, nt),
        in_specs=[pl.BlockSpec((bm, kf), lambda i, j: (i, 0)),
                  pl.BlockSpec((None, kf, bn), lambda i, j, l=layer: (l, 0, j)),
                  row,
                  pl.BlockSpec((None, d, bn), lambda i, j, l=layer: (l, 0, j)),
                  pl.BlockSpec((bm, D_PLE), lambda i, j: (i, 0)),
                  pl.BlockSpec((None, D_PLE, bn), lambda i, j, l=layer: (l, 0, j)),
                  pl.BlockSpec((bm, bn), lambda i, j: (i, j)),
                  vec, vec],
        out_specs=(row, row),
        scratch_shapes=[pltpu.VMEM((nt, bm, bn), F32)],
        compiler_params=_params("parallel", "arbitrary"),
        name="down_ln2",
    )(act, w_down, x1_bf, wpg, p, wpe, x1, g.reshape(1, d), b.reshape(1, d))


def _pad_cast_kernel(a_ref, o_ref):
    n = a_ref.shape[1]
    o_ref[:, :n] = a_ref[...].astype(BF16)
    o_ref[:, n:] = jnp.zeros((o_ref.shape[0], o_ref.shape[1] - n), BF16)


def _pad_cast_second_half(w, n_pad, rb=256):
    layers, k, n2 = w.shape
    n = n2 // 2
    assert n % LANES == 0 and n_pad % LANES == 0 and k % rb == 0
    return pl.pallas_call(
        _pad_cast_kernel,
        out_shape=jax.ShapeDtypeStruct((layers, k, n_pad), BF16),
        grid=(layers, k // rb),
        in_specs=[pl.BlockSpec((None, rb, n), lambda l, r: (l, r, 1))],
        out_specs=pl.BlockSpec((None, rb, n_pad), lambda l, r: (l, r, 0)),
        compiler_params=_params("parallel", "parallel"),
        name="pad_cast",
    )(w)


def _pad_rows(w, rows):
    return jnp.pad(w, ((0, rows - w.shape[0]), (0, 0)))


def _pad_cols(w, cols):
    return jnp.pad(w, ((0, 0),) * (w.ndim - 1) + ((0, cols - w.shape[-1]),))


def _layer_weights(i, big, conv_a_w, conv_b_w, a_log, dt_bias, norm_b_g, ln_c_g, ln_c_b, w_s, b_s,
                   ln1_g, ln1_b, conv_f_w, ln2_g, ln2_b):
    w = dict(big)
    w["layer"] = i
    w["conv_a"] = _pad_rows(conv_a_w[i], SUBLANES)
    w["conv_b"] = _pad_cols(_pad_rows(conv_b_w[i], SUBLANES), D_QKVZ)
    w["conv_f"] = _pad_cols(_pad_rows(conv_f_w[i], SUBLANES), D_FF_PAD)
    on_decay_lanes = ((0, 0), (H_B, LANES - 2 * H_B))
    w["alog"] = jnp.pad(a_log[i].reshape(1, H_B), on_decay_lanes)
    w["dtb"] = jnp.pad(dt_bias[i].reshape(1, H_B), on_decay_lanes)
    w["ng"] = norm_b_g[i].reshape(1, DV)
    w["lncg"] = ln_c_g[i].reshape(1, D_C)
    w["lncb"] = ln_c_b[i].reshape(1, D_C)
    w["ws"] = w_s[i]
    w["bias_c"] = jnp.repeat(b_s[i].T, D_C // G_C, axis=1)
    w["ln1g"], w["ln1b"] = ln1_g[i], ln1_b[i]
    w["ln2g"], w["ln2b"] = ln2_g[i], ln2_b[i]
    return w


def _stacked_weights(w_in, w_a_out, w_b_out, w_c_out, w_o, w_up, w_down, w_pe, w_pg):
    big = {}
    big["w_in_t"] = jnp.swapaxes(w_in, 1, 2).reshape(w_in.shape[0] * N_IN, w_in.shape[1])
    big["a_out"], big["b_out"], big["c_out"] = w_a_out, w_b_out, w_c_out
    big["o"] = w_o.astype(BF16)
    big["w_up"] = w_up
    big["up_u"] = _pad_cast_second_half(w_up, D_FF_PAD)
    big["down"] = w_down.astype(BF16)
    big["pe"] = w_pe.astype(BF16)
    big["pg"] = w_pg.astype(BF16)
    return big


def _sample_mix_coeffs(w, steps):
    cg = D_C // G_C
    wc = jnp.transpose(w["ws"][:, :steps, :steps], (1, 2, 0))
    wc = jnp.repeat(wc, cg, axis=2).reshape(steps * steps, D_C)
    return wc, w["bias_c"][:SUBLANES]


def _layer(x, x_bf, p2d, w, rows, hists, delta_fn, mixc_fn):
    hist_a, hist_q, hist_f = hists
    layer = w["layer"]
    ya, tail_a = _conv_call(_mixa_kernel, rows, CONV_A, x_bf,
                            [(w["w_in_t"], 0), (w["w_in_t"], D_A), (w["w_in_t"], 2 * D_A)], layer,
                            hist_a, w["conv_a"], D_A, 256, BF16, "mixa", wt=True)
    qkvz, tail_q = _conv_call(_qkvz_kernel, rows, CONV_B, x_bf, [(w["w_in_t"], OFF_QKV)], layer,
                              hist_q, w["conv_b"], D_QKVZ, 1024, F32, "qkvz", wt=True)
    small = _mm(x_bf, w["w_in_t"], layer, OFF_BETA, LANES, rows.bm, LANES, F32, "in_proj_small",
                wt=True)
    yb, s_fin = delta_fn(qkvz, small)
    yc, *vc = mixc_fn(x_bf)

    merged = _merge(x_bf, ya, yb, yc, w["w_in_t"], w["a_out"], w["b_out"], w["c_out"], layer, rows.bm)
    x1, x1_bf = _o_ln1(merged, w["o"], layer, x, w["ln1g"], w["ln1b"])
    act, tail_f = _conv_call(_ffn_up_kernel, rows, CONV_F, x1_bf,
                             [(w["w_up"], 0), (w["up_u"], 0)], layer,
                             hist_f, w["conv_f"], D_FF_PAD, 512, BF16, "ffn_up")
    x2, x2_bf = _down_ln2(act, w["down"], x1, x1_bf, p2d, w["pg"], w["pe"], layer,
                          w["ln2g"], w["ln2b"], 512)
    return x2, x2_bf, tail_a, tail_q, s_fin, tail_f, vc


def _tm(a):
    return jnp.swapaxes(a, 0, 1)


def _layer_prompt(x, x_bf, p2d, w, nb, seq):
    rows = _Rows(bm=1024, stride=1, bps=seq // 1024, recompute=True)
    m = x.shape[0]
    hists = tuple(jnp.zeros((nb, BF16_ROWS, c), F32) for c in (D_A, D_QKVZ, D_FF_PAD))

    def delta_fn(qkvz, small):
        yb, s_fin = _delta(qkvz.reshape(nb, seq, D_QKVZ), small.reshape(nb, seq, LANES),
                           w["alog"], w["dtb"], w["ng"], jnp.zeros((1, nb, H_B, DK, DV), F32), 0,
                           4, PROMPT_CHUNK, PROMPT_CHUNK, SUBLANES, "delta_prompt")
        return yb.reshape(m, D_B), s_fin

    def mixc_fn(xb):
        return _mixc(xb, w["w_in_t"], w["layer"], w["lncg"], w["lncb"], w["ws"], w["bias_c"], 512, 0,
                     "mixc_prompt")

    x2, x2_bf, tail_a, tail_q, s_fin, tail_f, _ = _layer(x, x_bf, p2d, w, rows, hists, delta_fn, mixc_fn)
    last = slice(rows.bps - 1, None, rows.bps)
    new_a = tail_a[last, SUBLANES - (CONV_A - 1):, :]
    new_q = tail_q[last, SUBLANES - (CONV_B - 1):, :3 * D_B]
    new_f = tail_f[last, SUBLANES - (CONV_F - 1):, :D_FF]
    return x2, x2_bf, new_a, new_q, s_fin, new_f


def _layer_sample(x, x_bf, p2d, w, st_a, st_q, st_d_all, new_st_d, st_f, nb, steps):
    m = x.shape[0]
    layer = w["layer"]
    rows = _Rows(bm=m, stride=nb, bps=1, recompute=False)
    chunk = 2 * SUBLANES

    def hist(st, cols):
        h = _tm(st)
        return _pad_cols(h.reshape(1, h.shape[0] * nb, h.shape[2]), cols)

    hists = (hist(st_a, D_A), hist(st_q, D_QKVZ), hist(st_f, D_FF_PAD))

    def to_chunks(a2d):
        a = _tm(a2d.reshape(steps, nb, a2d.shape[1]))
        return jnp.pad(a, ((0, 0), (0, chunk - steps), (0, 0)))

    def delta_fn(qkvz, small):
        yb_b, s_fin = _delta(to_chunks(qkvz), to_chunks(small), w["alog"], w["dtb"], w["ng"],
                             st_d_all, layer, 8, chunk, steps, SUBLANES, "delta_sample",
                             stacked="first" if new_st_d is None else new_st_d)
        return _tm(yb_b[:, :steps]).reshape(m, D_B), s_fin

    def mixc_fn(xb):
        wc, bias = _sample_mix_coeffs(w, steps)
        return _mixc(xb, w["w_in_t"], w["layer"], w["lncg"], w["lncb"], wc, bias, m, steps, "mixc_sample")

    x2, x2_bf, tail_a, tail_q, s_fin, tail_f, vc = _layer(x, x_bf, p2d, w, rows, hists, delta_fn, mixc_fn)

    def from_tail(t, cols):
        return _tm(t.reshape(t.shape[1] // nb, nb, t.shape[2]))[:, :, :cols]

    new_v = _tm(vc[0].reshape(steps, nb, D_C))
    return (x2, x2_bf, from_tail(tail_a, D_A), from_tail(tail_q, 3 * D_B), s_fin,
            from_tail(tail_f, D_FF), new_v)


def kernel(x_prompt, x_sample, state_conv_a, state_conv_qkv, state_delta, state_conv_ffn,
           p_prompt, p_sample, ln_in_g, ln_in_b,
           w_in, conv_a_w, w_a_out, conv_b_w, a_log, dt_bias, norm_b_g, w_b_out,
           ln_c_g, ln_c_b, w_s, b_s, w_c_out, w_o, ln1_g, ln1_b,
           w_up, conv_f_w, w_down, w_pe, w_pg, ln2_g, ln2_b):
    nbp, seq, d = x_prompt.shape
    nbs, steps, _ = x_sample.shape
    mp, ms = nbp * seq, nbs * steps

    xp, xp_bf = _ln_in(x_prompt.reshape(mp, d), ln_in_g, ln_in_b, 256, "ln_in")
    xs, xs_bf = _ln_in(_tm(x_sample).reshape(ms, d), ln_in_g, ln_in_b, 256, "ln_in_s")

    outs_p = [[] for _ in range(4)]
    outs_s = [[] for _ in range(5)]
    new_st_d = None
    big = _stacked_weights(w_in, w_a_out, w_b_out, w_c_out, w_o, w_up, w_down, w_pe, w_pg)
    for i in range(DEPTH):
        w = _layer_weights(i, big, conv_a_w, conv_b_w, a_log, dt_bias, norm_b_g, ln_c_g, ln_c_b,
                           w_s, b_s, ln1_g, ln1_b, conv_f_w, ln2_g, ln2_b)
        xp, xp_bf, *st_p = _layer_prompt(xp, xp_bf, p_prompt[i].reshape(mp, D_PLE), w, nbp, seq)
        for acc, val in zip(outs_p, st_p):
            acc.append(val)
        xs, xs_bf, *st_s = _layer_sample(xs, xs_bf, _tm(p_sample[i]).reshape(ms, D_PLE), w,
                                         state_conv_a[i], state_conv_qkv[i], state_delta, new_st_d,
                                         state_conv_ffn[i], nbs, steps)
        new_st_d = st_s[2]
        for acc, val in zip(outs_s, st_s):
            acc.append(val)

    y_prompt = xp.reshape(nbp, seq, d)
    y_sample = _tm(xs.reshape(steps, nbs, d))
    return (y_prompt, y_sample,
            jnp.stack(outs_p[0]), jnp.stack(outs_p[1]), jnp.stack(outs_p[2]), jnp.stack(outs_p[3]),
            jnp.stack(outs_s[0]), jnp.stack(outs_s[1]), new_st_d, jnp.stack(outs_s[3]),
            jnp.stack(outs_s[4]))
```
